```python
import jax, jax.numpy as jnp
from jax import lax
import numpy as np

D_MODEL = 2048
BATCH = 2
SEQ = 4096
DEPTH = 1

HEAD_DIM = 64
N_Q_HEADS = 16
N_KV_HEADS = 4
GROUP = N_Q_HEADS // N_KV_HEADS
WINDOW = 128
Q_W = N_Q_HEADS * HEAD_DIM
KV_W = N_KV_HEADS * HEAD_DIM
POOL_WINDOWS = (2, 4, 8, 16)
N_POOL_GROUPS = len(POOL_WINDOWS)
POOL_W = D_MODEL // 2
POOL_GC = POOL_W // N_POOL_GROUPS
N_BRANCHES = 2
IN_W = Q_W + 2 * KV_W + POOL_W + N_BRANCHES * D_MODEL
N_EXPERTS = 32
TOP_K = 4
D_FF = D_MODEL
SWIGLU_LIMIT = 7.0
SWIGLU_ALPHA = 1.702
EXPERT_BLOCK = 256
N_MOD = 6
RMS_EPS = 1e-6
NEG_INF = -1e30

kernel_name = "hybrid_swa_sink_pool_moe_block"


def rms_norm(x, g):
    xf = x.astype(jnp.float32)
    y = xf * lax.rsqrt(jnp.mean(xf * xf, axis=-1, keepdims=True) + RMS_EPS)
    return (y * g.astype(jnp.float32)).astype(x.dtype)


def sliding_window_sink_attention(q, k, v, sinks):
    B, S, _ = q.shape
    nb = S // WINDOW
    qb = q.reshape(B, nb, WINDOW, N_KV_HEADS, GROUP, HEAD_DIM)

    def band_blocks(t):
        t = t.reshape(B, S, N_KV_HEADS, HEAD_DIM)
        tp = jnp.pad(t, ((0, 0), (WINDOW, 0), (0, 0), (0, 0)))
        prev = tp[:, :S].reshape(B, nb, WINDOW, N_KV_HEADS, HEAD_DIM)
        cur = tp[:, WINDOW:].reshape(B, nb, WINDOW, N_KV_HEADS, HEAD_DIM)
        return jnp.concatenate([prev, cur], axis=2)

    kb, vb = band_blocks(k), band_blocks(v)
    s = jnp.einsum('bnqhgd,bnkhd->bnhgqk', qb, kb).astype(jnp.float32) * (HEAD_DIM ** -0.5)
    qpos = jnp.arange(WINDOW)[:, None] + WINDOW
    kpos = jnp.arange(2 * WINDOW)[None, :]
    diff = qpos - kpos
    band = (diff >= 0) & (diff < WINDOW)
    valid = (jnp.arange(nb)[:, None] > 0) | (jnp.arange(2 * WINDOW)[None, :] >= WINDOW)
    mask = band[None, :, :] & valid[:, None, :]
    s = jnp.where(mask[None, :, None, None, :, :], s, NEG_INF)
    sink = jnp.broadcast_to(
        sinks.astype(jnp.float32).reshape(N_KV_HEADS, GROUP)[None, None, :, :, None, None],
        s.shape[:-1] + (1,))
    p = jax.nn.softmax(jnp.concatenate([s, sink], axis=-1), axis=-1)[..., :-1]
    o = jnp.einsum('bnhgqk,bnkhd->bnqhgd', p.astype(vb.dtype), vb)
    return o.reshape(B, S, Q_W)


def multiscale_causal_pool(u):
    B, S, _ = u.shape
    ug = u.reshape(B, S, N_POOL_GROUPS, POOL_GC).astype(jnp.float32)
    cs = jnp.cumsum(ug, axis=1)
    t = jnp.arange(S)
    outs = []
    for g, w in enumerate(POOL_WINDOWS):
        csg = cs[:, :, g]
        prev = jnp.pad(csg, ((0, 0), (w, 0), (0, 0)))[:, :S]
        cnt = jnp.minimum(t + 1, w).astype(jnp.float32)[None, :, None]
        outs.append((csg - prev) / cnt)
    mean = jnp.stack(outs, axis=2)
    return (mean - ug).astype(u.dtype)


def clamped_swiglu(gu):
    x_glu = jnp.minimum(gu[..., ::2], SWIGLU_LIMIT)
    x_lin = jnp.clip(gu[..., 1::2], -SWIGLU_LIMIT, SWIGLU_LIMIT)
    return x_glu * jax.nn.sigmoid(SWIGLU_ALPHA * x_glu) * (x_lin + 1.0)


def moe_ffn(h, w_router, b_router, w_gate_up, b_gate_up, w_down, b_down):
    B, S, D = h.shape
    T = B * S
    hf = h.reshape(T, D)
    logits = (hf @ w_router + b_router).astype(jnp.float32)
    top_val, top_idx = lax.top_k(logits, TOP_K)
    gate_w = jax.nn.softmax(top_val, axis=-1)
    n_assign = T * TOP_K
    flat_e = top_idx.reshape(-1)
    flat_tok = jnp.repeat(jnp.arange(T, dtype=jnp.int32), TOP_K)
    flat_w = gate_w.reshape(-1)
    order = jnp.argsort(flat_e)
    se = flat_e[order]
    counts = jnp.bincount(flat_e, length=N_EXPERTS)
    pcounts = (counts + EXPERT_BLOCK - 1) // EXPERT_BLOCK * EXPERT_BLOCK
    pend = jnp.cumsum(pcounts)
    pstart = pend - pcounts
    ustart = jnp.cumsum(counts) - counts
    dest = pstart[se] + jnp.arange(n_assign) - ustart[se]
    nblk = -(-n_assign // EXPERT_BLOCK) + N_EXPERTS
    npad = nblk * EXPERT_BLOCK
    row_tok = jnp.zeros((npad,), jnp.int32).at[dest].set(flat_tok[order])
    row_w = jnp.zeros((npad,), jnp.float32).at[dest].set(flat_w[order])
    blk_e = jnp.minimum(
        jnp.searchsorted(pend, jnp.arange(nblk) * EXPERT_BLOCK, side='right'), N_EXPERTS - 1)
    xb = hf[row_tok].reshape(nblk, EXPERT_BLOCK, D)

    def expert_block(args):
        xblk, e = args
        gu = xblk @ w_gate_up[e] + b_gate_up[e]
        return clamped_swiglu(gu) @ w_down[e] + b_down[e]

    yb = lax.map(expert_block, (xb, blk_e)).reshape(npad, D)
    y = jnp.zeros((T, D), jnp.float32).at[row_tok].add(yb.astype(jnp.float32) * row_w[:, None])
    return y.astype(h.dtype).reshape(B, S, D)


def setup_inputs(seed: int = 0) -> dict:
    key = jax.random.key(seed)
    ks = jax.random.split(key, 24)
    D, L = D_MODEL, DEPTH
    nrm = lambda k, shape, s: jax.random.normal(k, shape, jnp.float32) * s
    return {
        "x": nrm(ks[0], (BATCH, SEQ, D), 1.0),
        "c": nrm(ks[1], (BATCH, D), 1.0),
        "w_mod": nrm(ks[2], (L, D, N_MOD * D), D ** -0.5),
        "b_mod": nrm(ks[3], (L, N_MOD * D), 0.02),
        "g_pre_mix": 1.0 + nrm(ks[4], (L, D), 0.05),
        "g_post_mix": 1.0 + nrm(ks[5], (L, D), 0.05),
        "w_in": nrm(ks[6], (L, D, IN_W), D ** -0.5),
        "b_in": nrm(ks[7], (L, IN_W), 0.02),
        "attn_sinks": nrm(ks[8], (L, N_Q_HEADS), 1.0),
        "w_pool_mix": nrm(ks[9], (L, N_POOL_GROUPS, POOL_GC, POOL_GC), POOL_GC ** -0.5),
        "pool_scale": 1.0 + nrm(ks[10], (L, POOL_W), 0.1),
        "w_attn_branch": nrm(ks[11], (L, Q_W, D), Q_W ** -0.5),
        "w_pool_branch": nrm(ks[12], (L, POOL_W, D), POOL_W ** -0.5),
        "w_out": nrm(ks[13], (L, D, D), D ** -0.5),
        "g_pre_ffn": 1.0 + nrm(ks[14], (L, D), 0.05),
        "g_post_ffn": 1.0 + nrm(ks[15], (L, D), 0.05),
        "w_router": nrm(ks[16], (L, D, N_EXPERTS), D ** -0.5),
        "b_router": nrm(ks[17], (L, N_EXPERTS), 0.01),
        "w_gate_up": nrm(ks[18], (L, N_EXPERTS, D, 2 * D_FF), D ** -0.5),
        "b_gate_up": nrm(ks[19], (L, N_EXPERTS, 2 * D_FF), 0.02),
        "w_down": nrm(ks[20], (L, N_EXPERTS, D_FF, D), D_FF ** -0.5),
        "b_down": nrm(ks[21], (L, N_EXPERTS, D), 0.02),
    }


def reference(x, c, w_mod, b_mod, g_pre_mix, g_post_mix, w_in, b_in, attn_sinks,
              w_pool_mix, pool_scale, w_attn_branch, w_pool_branch, w_out,
              g_pre_ffn, g_post_ffn, w_router, b_router, w_gate_up, b_gate_up,
              w_down, b_down):
    B, S, D = x.shape
    split_pts = [Q_W, Q_W + KV_W, Q_W + 2 * KV_W, Q_W + 2 * KV_W + POOL_W]
    for l in range(DEPTH):
        mod = jax.nn.silu(c) @ w_mod[l] + b_mod[l]
        shift_a, scale_a, gate_a, shift_f, scale_f, gate_f = [
            m[:, None, :] for m in jnp.split(mod, N_MOD, axis=-1)]

        h = rms_norm(x, g_pre_mix[l]) * (1.0 + scale_a) + shift_a
        proj = h @ w_in[l] + b_in[l]
        q, k, v, u, gate_logits = jnp.split(proj, split_pts, axis=-1)
        a = sliding_window_sink_attention(q, k, v, attn_sinks[l])
        pg = multiscale_causal_pool(u).reshape(B, S, N_POOL_GROUPS, POOL_GC)
        p = jnp.einsum('bsgc,gcd->bsgd', pg, w_pool_mix[l]).reshape(B, S, POOL_W) * pool_scale[l]
        gates = jax.nn.sigmoid(gate_logits).reshape(B, S, N_BRANCHES, D)
        merged = gates[:, :, 0] * (a @ w_attn_branch[l]) + gates[:, :, 1] * (p @ w_pool_branch[l])
        mix_out = merged @ w_out[l]
        x = x + gate_a * rms_norm(mix_out, g_post_mix[l])

        h = rms_norm(x, g_pre_ffn[l]) * (1.0 + scale_f) + shift_f
        y = moe_ffn(h, w_router[l], b_router[l], w_gate_up[l], b_gate_up[l], w_down[l], b_down[l])
        x = x + gate_f * rms_norm(y, g_post_ffn[l])
    return x
```

```python
import functools

import jax
import jax.numpy as jnp
from jax import lax
from jax.experimental import pallas as pl
from jax.experimental.pallas import tpu as pltpu

F32 = jnp.float32
BF16 = jnp.bfloat16

D_MODEL = 2048
HEAD_DIM = 64
N_Q_HEADS = 16
N_KV_HEADS = 4
GROUP = N_Q_HEADS // N_KV_HEADS
WINDOW = 128
Q_W = N_Q_HEADS * HEAD_DIM
KV_W = N_KV_HEADS * HEAD_DIM
POOL_WINDOWS = (2, 4, 8, 16)
POOL_W = D_MODEL // 2
POOL_GC = POOL_W // len(POOL_WINDOWS)
POOL_HALO = 16
N_EXPERTS = 32
TOP_K = 4
_LOG2_TOP_K = 2
D_FF = D_MODEL
SWIGLU_LIMIT = 7.0
SWIGLU_ALPHA = 1.702
N_MOD = 6
RMS_EPS = 1e-6
NEG_INF = -1e30

LANES = 128
SUBLANES = 8
VMEM_LIMIT_BYTES = 56 * 1024 * 1024

MOD_TN = 1024
PROJ_TM = 1024
PROJ_TN = 512
ATT_TQ = 512
MIX_TM = 256
MOE_R = 256
MOE_TMAX = 1536
MOE_FC = 256
DISP_TM = 256
COMB_TM = 128


def _cparams(sem):
    return pltpu.CompilerParams(dimension_semantics=sem, vmem_limit_bytes=VMEM_LIMIT_BYTES)


def _rms(x):
    return x * lax.rsqrt(jnp.mean(x * x, axis=-1, keepdims=True) + RMS_EPS)


def _mod_kernel(c_ref, w_ref, b_ref, o_ref):
    c = c_ref[...]
    s = c * jax.nn.sigmoid(c)
    o_ref[...] = jnp.dot(s.astype(BF16), w_ref[...].astype(BF16),
                         preferred_element_type=F32) + b_ref[...]


def _modulation(c_pad, w_mod, b_mod):
    rows, d = c_pad.shape
    n = w_mod.shape[1]
    return pl.pallas_call(
        _mod_kernel,
        grid=(n // MOD_TN,),
        in_specs=[
            pl.BlockSpec((rows, d), lambda j: (0, 0)),
            pl.BlockSpec((d, MOD_TN), lambda j: (0, j)),
            pl.BlockSpec((1, MOD_TN), lambda j: (0, j)),
        ],
        out_specs=pl.BlockSpec((rows, MOD_TN), lambda j: (0, j)),
        out_shape=jax.ShapeDtypeStruct((rows, n), F32),
        compiler_params=_cparams(("arbitrary",)),
        name="modulation",
    )(c_pad, w_mod, b_mod)


_Q_TILES = Q_W // PROJ_TN
_KV_TILES = (2 * KV_W) // PROJ_TN
_U_TILES = POOL_W // PROJ_TN
_G_TILES = (2 * D_MODEL) // PROJ_TN
_KV_T0 = _Q_TILES
_U_T0 = _KV_T0 + _KV_TILES
_G_T0 = _U_T0 + _U_TILES
_PROJ_TILES = _G_T0 + _G_TILES


def _proj_kernel(x_ref, mod_ref, g_ref, w_ref, b_ref, q_ref, kv_ref, u_ref, gl_ref, h_ref):
    j = pl.program_id(1)

    @pl.when(j == 0)
    def _():
        m = mod_ref[0]
        h = _rms(x_ref[...]) * g_ref[...] * (1.0 + m[1:2, :]) + m[0:1, :]
        h_ref[...] = h.astype(BF16)

    y = jnp.dot(h_ref[...], w_ref[...].astype(BF16), preferred_element_type=F32) + b_ref[...]

    @pl.when(j < _KV_T0)
    def _():
        q_ref[...] = y.astype(BF16)

    @pl.when((j >= _KV_T0) & (j < _U_T0))
    def _():
        kv_ref[...] = y.astype(BF16)

    @pl.when((j >= _U_T0) & (j < _G_T0))
    def _():
        u_ref[...] = y

    @pl.when(j >= _G_T0)
    def _():
        gl_ref[...] = y


def _in_projection(x2, mod6, g_pre, w_in, b_in, seq):
    t, d = x2.shape
    tiles_per_seq = seq // PROJ_TM

    def clampj(j, lo, n):
        return jnp.clip(j - lo, 0, n - 1)

    return pl.pallas_call(
        _proj_kernel,
        grid=(t // PROJ_TM, _PROJ_TILES),
        in_specs=[
            pl.BlockSpec((PROJ_TM, d), lambda i, j: (i, 0)),
            pl.BlockSpec((1, N_MOD, d), lambda i, j: (i // tiles_per_seq, 0, 0)),
            pl.BlockSpec((1, d), lambda i, j: (0, 0)),
            pl.BlockSpec((d, PROJ_TN), lambda i, j: (0, j)),
            pl.BlockSpec((1, PROJ_TN), lambda i, j: (0, j)),
        ],
        out_specs=[
            pl.BlockSpec((PROJ_TM, PROJ_TN), lambda i, j: (i, clampj(j, 0, _Q_TILES))),
            pl.BlockSpec((PROJ_TM, PROJ_TN), lambda i, j: (i, clampj(j, _KV_T0, _KV_TILES))),
            pl.BlockSpec((PROJ_TM, PROJ_TN), lambda i, j: (i, clampj(j, _U_T0, _U_TILES))),
            pl.BlockSpec((PROJ_TM, PROJ_TN), lambda i, j: (i, clampj(j, _G_T0, _G_TILES))),
        ],
        out_shape=[
            jax.ShapeDtypeStruct((t, Q_W), BF16),
            jax.ShapeDtypeStruct((t, 2 * KV_W), BF16),
            jax.ShapeDtypeStruct((t, POOL_W), F32),
            jax.ShapeDtypeStruct((t, 2 * D_MODEL), F32),
        ],
        scratch_shapes=[pltpu.VMEM((PROJ_TM, d), BF16)],
        compiler_params=_cparams(("arbitrary", "arbitrary")),
        name="in_projection",
    )(x2, mod6, g_pre, w_in, b_in)


def _attn_kernel(sink_ref, q_ref, kvp_ref, kvc_ref, o_ref, *, blocks_per_seq):
    i = pl.program_id(0)
    nq = ATT_TQ // WINDOW
    rows = GROUP * WINDOW
    r_iota = lax.broadcasted_iota(jnp.int32, (rows, 2 * WINDOW), 0)
    k_iota = lax.broadcasted_iota(jnp.int32, (rows, 2 * WINDOW), 1)
    qi = r_iota % WINDOW
    band = (k_iota > qi) & (k_iota <= qi + WINDOW)
    g_of_row = lax.broadcasted_iota(jnp.int32, (rows, 1), 0) // WINDOW
    scale = HEAD_DIM ** -0.5

    for n in range(nq):
        if n == 0:
            first = (i % blocks_per_seq) == 0
            mask = band & ((k_iota >= WINDOW) | jnp.logical_not(first))
            kv_prev = kvp_ref[...]
        else:
            mask = band
            kv_prev = kvc_ref[(n - 1) * WINDOW:n * WINDOW, :]
        kv_cur = kvc_ref[n * WINDOW:(n + 1) * WINDOW, :]
        kv = jnp.concatenate([kv_prev, kv_cur], axis=0)
        qn = q_ref[n * WINDOW:(n + 1) * WINDOW, :]
        for h in range(N_KV_HEADS):
            k = kv[:, h * HEAD_DIM:(h + 1) * HEAD_DIM]
            v = kv[:, KV_W + h * HEAD_DIM:KV_W + (h + 1) * HEAD_DIM]
            qs = jnp.concatenate(
                [qn[:, (h * GROUP + g) * HEAD_DIM:(h * GROUP + g + 1) * HEAD_DIM]
                 for g in range(GROUP)], axis=0)
            s = lax.dot_general(qs, k, (((1,), (1,)), ((), ())),
                                preferred_element_type=F32) * scale
            s = jnp.where(mask, s, NEG_INF)
            sink = jnp.zeros((rows, 1), F32)
            for g in range(GROUP):
                sink = jnp.where(g_of_row == g, sink_ref[h * GROUP + g], sink)
            m = jnp.maximum(jnp.max(s, axis=-1, keepdims=True), sink)
            p = jnp.exp(s - m)
            denom = jnp.sum(p, axis=-1, keepdims=True) + jnp.exp(sink - m)
            o = jnp.dot(p.astype(BF16), v, preferred_element_type=F32) / denom
            for g in range(GROUP):
                c0 = (h * GROUP + g) * HEAD_DIM
                o_ref[n * WINDOW:(n + 1) * WINDOW, c0:c0 + HEAD_DIM] = (
                    o[g * WINDOW:(g + 1) * WINDOW, :].astype(BF16))


def _attention(q, kv, sinks, seq):
    t = q.shape[0]
    blocks_per_seq = seq // ATT_TQ
    wpb = ATT_TQ // WINDOW
    return pl.pallas_call(
        functools.partial(_attn_kernel, blocks_per_seq=blocks_per_seq),
        grid=(t // ATT_TQ,),
        in_specs=[
            pl.BlockSpec(memory_space=pltpu.SMEM),
            pl.BlockSpec((ATT_TQ, Q_W), lambda i: (i, 0)),
            pl.BlockSpec((WINDOW, 2 * KV_W), lambda i: (jnp.maximum(i * wpb - 1, 0), 0)),
            pl.BlockSpec((ATT_TQ, 2 * KV_W), lambda i: (i, 0)),
        ],
        out_specs=pl.BlockSpec((ATT_TQ, Q_W), lambda i: (i, 0)),
        out_shape=jax.ShapeDtypeStruct((t, Q_W), BF16),
        compiler_params=_cparams(("arbitrary",)),
        name="swa_attention",
    )(sinks, q, kv, kv)


def _mix_kernel(x_ref, a_ref, up_ref, u_ref, gl_ref, mod_ref, wpm_ref, ps_ref, wa_ref, wp_ref,
                wo_ref, gpost_ref, gffn_ref, wr_ref, br_ref,
                x1_ref, h2_ref, idx_ref, gw_ref, ubuf_ref, *, tiles_per_seq):
    i = pl.program_id(0)
    tm = MIX_TM
    seq_tile = i % tiles_per_seq
    m = mod_ref[0]

    halo = jnp.where(seq_tile == 0, 0.0, up_ref[...])
    ubuf_ref[0:POOL_HALO, :] = halo
    ubuf_ref[POOL_HALO:, :] = u_ref[...]
    pos = seq_tile * tm + lax.broadcasted_iota(jnp.int32, (tm, 1), 0)
    p_parts = []
    for g, w in enumerate(POOL_WINDOWS):
        c0 = g * POOL_GC
        acc = ubuf_ref[POOL_HALO:POOL_HALO + tm, c0:c0 + POOL_GC]
        ug = acc
        for dlt in range(1, w):
            acc = acc + ubuf_ref[POOL_HALO - dlt:POOL_HALO - dlt + tm, c0:c0 + POOL_GC]
        cnt = jnp.minimum(pos + 1, w).astype(F32)
        pg = acc / cnt - ug
        p_parts.append(jnp.dot(pg.astype(BF16), wpm_ref[g], preferred_element_type=F32))
    p = jnp.concatenate(p_parts, axis=-1) * ps_ref[...]

    ya = jnp.dot(a_ref[...], wa_ref[...], preferred_element_type=F32)
    yp = jnp.dot(p.astype(BF16), wp_ref[...], preferred_element_type=F32)
    gates_a = jax.nn.sigmoid(gl_ref[:, 0:D_MODEL])
    gates_p = jax.nn.sigmoid(gl_ref[:, D_MODEL:2 * D_MODEL])
    merged = gates_a * ya + gates_p * yp
    mix = jnp.dot(merged.astype(BF16), wo_ref[...], preferred_element_type=F32)
    x1 = x_ref[...] + m[2:3, :] * (_rms(mix) * gpost_ref[...])
    x1_ref[...] = x1

    h2 = _rms(x1) * gffn_ref[...] * (1.0 + m[4:5, :]) + m[3:4, :]
    h2_ref[...] = h2

    logits = jnp.dot(h2, wr_ref[...], preferred_element_type=F32,
                     precision=lax.Precision.HIGHEST) + br_ref[...]
    e_iota = lax.broadcasted_iota(jnp.int32, (tm, N_EXPERTS), 1)
    k_iota = lax.broadcasted_iota(jnp.int32, (tm, TOP_K), 1)
    vals = jnp.zeros((tm, TOP_K), F32)
    idxs = jnp.zeros((tm, TOP_K), jnp.int32)
    lg = logits
    for k in range(TOP_K):
        mx = jnp.max(lg, axis=-1, keepdims=True)
        ix = jnp.min(jnp.where(lg == mx, e_iota, N_EXPERTS), axis=-1, keepdims=True)
        vals = jnp.where(k_iota == k, mx, vals)
        idxs = jnp.where(k_iota == k, ix, idxs)
        lg = jnp.where(e_iota == ix, -jnp.inf, lg)
    ex = jnp.exp(vals - vals[:, 0:1])
    gw_ref[...] = ex / jnp.sum(ex, axis=-1, keepdims=True)
    idx_ref[...] = idxs


def _mix(x2, a, u, gl, mod6, wpm, ps, wa, wp, wo, gpost, gffn, wr, br, seq):
    t, d = x2.shape
    tm = MIX_TM
    tiles_per_seq = seq // tm
    hb = tm // POOL_HALO
    const2 = lambda i: (0, 0)
    return pl.pallas_call(
        functools.partial(_mix_kernel, tiles_per_seq=tiles_per_seq),
        grid=(t // tm,),
        in_specs=[
            pl.BlockSpec((tm, d), lambda i: (i, 0)),
            pl.BlockSpec((tm, Q_W), lambda i: (i, 0)),
            pl.BlockSpec((POOL_HALO, POOL_W), lambda i: (jnp.maximum(i * hb - 1, 0), 0)),
            pl.BlockSpec((tm, POOL_W), lambda i: (i, 0)),
            pl.BlockSpec((tm, 2 * d), lambda i: (i, 0)),
            pl.BlockSpec((1, N_MOD, d), lambda i: (i // tiles_per_seq, 0, 0)),
            pl.BlockSpec(wpm.shape, lambda i: (0, 0, 0), pipeline_mode=pl.Buffered(1)),
            pl.BlockSpec((1, POOL_W), const2),
            pl.BlockSpec(wa.shape, const2, pipeline_mode=pl.Buffered(1)),
            pl.BlockSpec(wp.shape, const2, pipeline_mode=pl.Buffered(1)),
            pl.BlockSpec(wo.shape, const2, pipeline_mode=pl.Buffered(1)),
            pl.BlockSpec((1, d), const2),
            pl.BlockSpec((1, d), const2),
            pl.BlockSpec(wr.shape, const2),
            pl.BlockSpec((1, N_EXPERTS), const2),
        ],
        out_specs=[
            pl.BlockSpec((tm, d), lambda i: (i, 0)),
            pl.BlockSpec((tm, d), lambda i: (i, 0)),
            pl.BlockSpec((tm, TOP_K), lambda i: (i, 0)),
            pl.BlockSpec((tm, TOP_K), lambda i: (i, 0)),
        ],
        out_shape=[
            jax.ShapeDtypeStruct((t, d), F32),
            jax.ShapeDtypeStruct((t, d), F32),
            jax.ShapeDtypeStruct((t, TOP_K), jnp.int32),
            jax.ShapeDtypeStruct((t, TOP_K), F32),
        ],
        scratch_shapes=[pltpu.VMEM((POOL_HALO + tm, POOL_W), F32)],
        compiler_params=_cparams(("arbitrary",)),
        name="mix_merge_router",
    )(x2, a, u, u, gl, mod6, wpm, ps, wa, wp, wo, gpost, gffn, wr, br)


def _dispatch_kernel(dest_ref, h_hbm, xs_hbm, sem):
    i = pl.program_id(0)
    n = DISP_TM * TOP_K

    def copy(j):
        tok = i * DISP_TM + lax.shift_right_logical(j, _LOG2_TOP_K)
        return pltpu.make_async_copy(h_hbm.at[pl.ds(tok, 1)],
                                     xs_hbm.at[pl.ds(dest_ref[0, 0, j], 1)], sem)

    def start(j, c):
        copy(j).start()
        return c

    def wait(j, c):
        copy(j).wait()
        return c

    lax.fori_loop(0, n, start, 0)
    lax.fori_loop(0, n, wait, 0)


def _dispatch(dest, h2, n_rows):
    t, d = h2.shape
    steps = t // DISP_TM
    dest3 = dest.reshape(steps, 1, DISP_TM * TOP_K)
    return pl.pallas_call(
        _dispatch_kernel,
        grid=(steps,),
        in_specs=[
            pl.BlockSpec((1, 1, DISP_TM * TOP_K), lambda i: (i, 0, 0), memory_space=pltpu.SMEM),
            pl.BlockSpec(memory_space=pl.ANY),
        ],
        out_specs=pl.BlockSpec(memory_space=pl.ANY),
        out_shape=jax.ShapeDtypeStruct((n_rows, d), F32),
        scratch_shapes=[pltpu.SemaphoreType.DMA(())],
        compiler_params=_cparams(("arbitrary",)),
        name="moe_dispatch",
    )(dest3, h2)


_MOE_NC = D_FF // MOE_FC
_GU_W = 2 * MOE_FC


def _moe_kernel(ie_ref, ir_ref, ins_ref, inv_ref, xs_hbm, w1_ref, b1_ref, w2_ref, b2_ref, y_hbm,
                xs_v, acc_v, w1_v, w2f_v, w2_v, sem):
    i = pl.program_id(0)

    @pl.when(ins_ref[i] > 0)
    def _():
        _moe_item(ir_ref[i], ins_ref[i], inv_ref[i], xs_hbm, w1_ref, b1_ref, w2_ref, b2_ref,
                  y_hbm, xs_v, acc_v, w1_v, w2f_v, w2_v, sem)


def _moe_item(row0, nsub, nvalid, xs_hbm, w1_ref, b1_ref, w2_ref, b2_ref, y_hbm,
              xs_v, acc_v, w1_v, w2f_v, w2_v, sem):
    c = pl.program_id(1)
    r = MOE_R

    def in_copy(s):
        return pltpu.make_async_copy(
            xs_hbm.at[pl.ds(pl.multiple_of(row0 + s * r, r), r)],
            acc_v.at[pl.ds(pl.multiple_of(s * r, r), r)], sem)

    def out_copy(s):
        return pltpu.make_async_copy(
            acc_v.at[pl.ds(pl.multiple_of(s * r, r), r)],
            y_hbm.at[pl.ds(pl.multiple_of(row0 + s * r, r), r)], sem)

    def each(fn):
        def body(s, carry):
            fn(s)
            return carry
        lax.fori_loop(0, nsub, body, 0)

    @pl.when(c == 0)
    def _():
        each(lambda s: in_copy(s).start())
        each(lambda s: in_copy(s).wait())

        def cast(s):
            sl = pl.ds(pl.multiple_of(s * r, r), r)
            live = (s * r + lax.broadcasted_iota(jnp.int32, (r, 1), 0)) < nvalid
            xs_v[sl, :] = jnp.where(live, acc_v[sl, :], 0.0).astype(BF16)
        each(cast)

    w1_v[...] = w1_ref[0].astype(BF16)
    half = LANES // 2
    for cb in range(D_MODEL // LANES):
        cs = slice(cb * LANES, (cb + 1) * LANES)
        for blk in range(MOE_FC // LANES):
            b0 = blk * LANES
            w2f_v[cb, pl.ds(b0, half, stride=2), :] = w2_ref[0, b0:b0 + half, cs]
            w2f_v[cb, pl.ds(b0 + 1, half, stride=2), :] = w2_ref[0, b0 + half:b0 + LANES, cs]
        w2_v[:, cs] = w2f_v[cb].astype(BF16)

    lane = lax.broadcasted_iota(jnp.int32, (r, LANES), 1)
    even = (lane % 2) == 0
    b1 = b1_ref[0]

    def sub(s):
        sl = pl.ds(pl.multiple_of(s * r, r), r)
        gu = jnp.dot(xs_v[sl, :], w1_v[...], preferred_element_type=F32) + b1
        glu = jnp.minimum(gu, SWIGLU_LIMIT)
        f = glu * jax.nn.sigmoid(SWIGLU_ALPHA * glu)
        lin = jnp.clip(gu, -SWIGLU_LIMIT, SWIGLU_LIMIT) + 1.0
        parts = []
        for blk in range(MOE_FC // LANES):
            ca = 2 * blk * LANES
            cb = ca + LANES
            za = f[:, ca:ca + LANES] * pltpu.roll(lin[:, ca:ca + LANES], LANES - 1, 1)
            zb = pltpu.roll(f[:, cb:cb + LANES], 1, 1) * lin[:, cb:cb + LANES]
            parts.append(jnp.where(even, za, zb))
        act = jnp.concatenate(parts, axis=-1).astype(BF16)
        contrib = jnp.dot(act, w2_v[...], preferred_element_type=F32)

        @pl.when(c == 0)
        def _():
            acc_v[sl, :] = contrib

        @pl.when(c != 0)
        def _():
            acc_v[sl, :] = acc_v[sl, :] + contrib

    each(sub)

    @pl.when(c == _MOE_NC - 1)
    def _():
        def bias(s):
            sl = pl.ds(pl.multiple_of(s * r, r), r)
            acc_v[sl, :] = acc_v[sl, :] + b2_ref[0]
        each(bias)
        each(lambda s: out_copy(s).start())
        each(lambda s: out_copy(s).wait())


def _moe(item_e, item_row0, item_nsub, item_nvalid, xs, w_gate_up, b_gate_up, w_down, b_down):
    n_items = item_e.shape[0]
    n_rows, d = xs.shape
    e = w_gate_up.shape[0]
    b1 = b_gate_up.reshape(e, 1, 2 * D_FF)
    b2 = b_down.reshape(e, 1, d)

    def chunk(i, c, ins):
        return jnp.where(ins[i] > 0, c, _MOE_NC - 1)

    grid_spec = pltpu.PrefetchScalarGridSpec(
        num_scalar_prefetch=4,
        grid=(n_items, _MOE_NC),
        in_specs=[
            pl.BlockSpec(memory_space=pl.ANY),
            pl.BlockSpec((1, d, _GU_W), lambda i, c, ie, ir, ins, inv: (ie[i], 0, chunk(i, c, ins))),
            pl.BlockSpec((1, 1, _GU_W), lambda i, c, ie, ir, ins, inv: (ie[i], 0, chunk(i, c, ins))),
            pl.BlockSpec((1, MOE_FC, d), lambda i, c, ie, ir, ins, inv: (ie[i], chunk(i, c, ins), 0)),
            pl.BlockSpec((1, 1, d), lambda i, c, ie, ir, ins, inv: (ie[i], 0, 0)),
        ],
        out_specs=pl.BlockSpec(memory_space=pl.ANY),
        scratch_shapes=[
            pltpu.VMEM((MOE_TMAX, d), BF16),
            pltpu.VMEM((MOE_TMAX, d), F32),
            pltpu.VMEM((d, _GU_W), BF16),
            pltpu.VMEM((d // LANES, MOE_FC, LANES), F32),
            pltpu.VMEM((MOE_FC, d), BF16),
            pltpu.SemaphoreType.DMA(()),
        ],
    )
    return pl.pallas_call(
        _moe_kernel,
        grid_spec=grid_spec,
        out_shape=jax.ShapeDtypeStruct((n_rows, d), F32),
        compiler_params=_cparams(("arbitrary", "arbitrary")),
        name="moe_experts",
    )(item_e, item_row0, item_nsub, item_nvalid, xs, w_gate_up, b1, w_down, b2)


def _combine_kernel(dest_ref, y_hbm, gw_ref, x1_ref, mod_ref, g_ref, o_ref, buf, sem):
    n = COMB_TM * TOP_K

    def copy(j):
        return pltpu.make_async_copy(
            y_hbm.at[pl.ds(dest_ref[0, 0, j], 1)],
            buf.at[j & (TOP_K - 1), pl.ds(lax.shift_right_logical(j, _LOG2_TOP_K), 1)], sem)

    def start(j, c):
        copy(j).start()
        return c

    def wait(j, c):
        copy(j).wait()
        return c

    lax.fori_loop(0, n, start, 0)
    lax.fori_loop(0, n, wait, 0)

    gw = gw_ref[...]
    y = buf[0] * gw[:, 0:1]
    for k in range(1, TOP_K):
        y = y + buf[k] * gw[:, k:k + 1]
    m = mod_ref[0]
    o_ref[...] = x1_ref[...] + m[5:6, :] * (_rms(y) * g_ref[...])


def _combine(dest, yb, gw, x1, mod6, g_post, seq):
    t, d = x1.shape
    steps = t // COMB_TM
    tiles_per_seq = seq // COMB_TM
    dest3 = dest.reshape(steps, 1, COMB_TM * TOP_K)
    return pl.pallas_call(
        _combine_kernel,
        grid=(steps,),
        in_specs=[
            pl.BlockSpec((1, 1, COMB_TM * TOP_K), lambda i: (i, 0, 0), memory_space=pltpu.SMEM),
            pl.BlockSpec(memory_space=pl.ANY),
            pl.BlockSpec((COMB_TM, TOP_K), lambda i: (i, 0)),
            pl.BlockSpec((COMB_TM, d), lambda i: (i, 0)),
            pl.BlockSpec((1, N_MOD, d), lambda i: (i // tiles_per_seq, 0, 0)),
            pl.BlockSpec((1, d), lambda i: (0, 0)),
        ],
        out_specs=pl.BlockSpec((COMB_TM, d), lambda i: (i, 0)),
        out_shape=jax.ShapeDtypeStruct((t, d), F32),
        scratch_shapes=[pltpu.VMEM((TOP_K, COMB_TM, d), F32), pltpu.SemaphoreType.DMA(())],
        compiler_params=_cparams(("arbitrary",)),
        name="moe_combine",
    )(dest3, yb, gw, x1, mod6, g_post)


def _routing_tables(top_idx, n_items):
    t = top_idx.shape[0]
    sel = jnp.sum((top_idx[:, :, None] == jnp.arange(N_EXPERTS, dtype=jnp.int32)[None, None, :])
                  .astype(jnp.int32), axis=1)
    incl = jnp.cumsum(sel, axis=0)
    rank = incl - sel
    counts = incl[-1]
    pcounts = (counts + MOE_R - 1) // MOE_R * MOE_R
    pend = jnp.cumsum(pcounts)
    pstart = pend - pcounts
    dest = pstart[top_idx] + jnp.take_along_axis(rank, top_idx, axis=1)

    tiles = (pcounts + MOE_TMAX - 1) // MOE_TMAX
    tend = jnp.cumsum(tiles)
    tstart = tend - tiles
    it = jnp.arange(n_items, dtype=jnp.int32)
    ie = jnp.minimum(jnp.sum((tend[None, :] <= it[:, None]).astype(jnp.int32), axis=1),
                     N_EXPERTS - 1)
    valid = it < tend[-1]
    local = it - tstart[ie]
    row0 = pstart[ie] + local * MOE_TMAX
    nrows = jnp.clip(pcounts[ie] - local * MOE_TMAX, 0, MOE_TMAX)
    nsub = jnp.where(valid, nrows // MOE_R, 0).astype(jnp.int32)
    nvalid = jnp.where(valid, jnp.clip(counts[ie] - local * MOE_TMAX, 0, MOE_TMAX), 0)
    row0 = jnp.where(valid, row0, 0).astype(jnp.int32)
    return dest.astype(jnp.int32), ie, row0, nsub, nvalid.astype(jnp.int32)


def kernel(x, c, w_mod, b_mod, g_pre_mix, g_post_mix, w_in, b_in, attn_sinks, w_pool_mix,
           pool_scale, w_attn_branch, w_pool_branch, w_out, g_pre_ffn, g_post_ffn, w_router,
           b_router, w_gate_up, b_gate_up, w_down, b_down):
    b, s, d = x.shape
    t = b * s
    depth = w_mod.shape[0]
    n_assign = t * TOP_K
    n_items = N_EXPERTS + n_assign // MOE_TMAX
    n_rows = n_assign + N_EXPERTS * MOE_R

    x2 = x.reshape(t, d)
    c_pad = jnp.pad(c, ((0, SUBLANES - b), (0, 0)))
    for l in range(depth):
        mod = _modulation(c_pad, w_mod[l], b_mod[l][None, :])
        mod6 = mod[:b].reshape(b, N_MOD, d)

        q, kv, u, gl = _in_projection(x2, mod6, g_pre_mix[l][None, :], w_in[l],
                                      b_in[l][None, :], s)
        a = _attention(q, kv, attn_sinks[l], s)
        x1, h2, top_idx, gate_w = _mix(
            x2, a, u, gl, mod6, w_pool_mix[l].astype(BF16), pool_scale[l][None, :],
            w_attn_branch[l].astype(BF16), w_pool_branch[l].astype(BF16),
            w_out[l].astype(BF16), g_post_mix[l][None, :], g_pre_ffn[l][None, :],
            w_router[l], b_router[l][None, :], s)

        dest, item_e, item_row0, item_nsub, item_nvalid = _routing_tables(top_idx, n_items)
        xs = _dispatch(dest, h2, n_rows)
        yb = _moe(item_e, item_row0, item_nsub, item_nvalid, xs, w_gate_up[l], b_gate_up[l],
                  w_down[l], b_down[l])
        x2 = _combine(dest, yb, gate_w, x1, mod6, g_post_ffn[l][None, :], s)
    return x2.reshape(b, s, d)
```

```python
import functools

import jax
import jax.numpy as jnp
from jax import lax
from jax.experimental import pallas as pl
from jax.experimental.pallas import tpu as pltpu

F32 = jnp.float32
BF16 = jnp.bfloat16

D_MODEL = 2048
HEAD_DIM = 64
N_Q_HEADS = 16
N_KV_HEADS = 4
GROUP = N_Q_HEADS // N_KV_HEADS
WINDOW = 128
Q_W = N_Q_HEADS * HEAD_DIM
KV_W = N_KV_HEADS * HEAD_DIM
POOL_WINDOWS = (2, 4, 8, 16)
POOL_W = D_MODEL // 2
POOL_GC = POOL_W // len(POOL_WINDOWS)
POOL_HALO = 16
N_EXPERTS = 32
TOP_K = 4
_LOG2_TOP_K = 2
D_FF = D_MODEL
SWIGLU_LIMIT = 7.0
SWIGLU_ALPHA = 1.702
N_MOD = 6
RMS_EPS = 1e-6
NEG_INF = -1e30

LANES = 128
SUBLANES = 8
VMEM_LIMIT_BYTES = 56 * 1024 * 1024

MOD_TN = 1024
PROJ_TM = 1024
PROJ_TN = 512
ATT_TQ = 512
MIX_TM = 256
MOE_R = 256
MOE_TMAX = 1536
MOE_FC = 256
DISP_TM = 256
COMB_TM = 128
DMA_UNROLL = 8


def _cparams(sem):
    return pltpu.CompilerParams(dimension_semantics=sem, vmem_limit_bytes=VMEM_LIMIT_BYTES)


def _rms(x):
    return x * lax.rsqrt(jnp.mean(x * x, axis=-1, keepdims=True) + RMS_EPS)


def _mod_kernel(c_ref, w_ref, b_ref, o_ref):
    c = c_ref[...]
    s = c * jax.nn.sigmoid(c)
    o_ref[...] = jnp.dot(s.astype(BF16), w_ref[...].astype(BF16),
                         preferred_element_type=F32) + b_ref[...]


def _modulation(c_pad, w_mod, b_mod):
    rows, d = c_pad.shape
    n = w_mod.shape[1]
    return pl.pallas_call(
        _mod_kernel,
        grid=(n // MOD_TN,),
        in_specs=[
            pl.BlockSpec((rows, d), lambda j: (0, 0)),
            pl.BlockSpec((d, MOD_TN), lambda j: (0, j)),
            pl.BlockSpec((1, MOD_TN), lambda j: (0, j)),
        ],
        out_specs=pl.BlockSpec((rows, MOD_TN), lambda j: (0, j)),
        out_shape=jax.ShapeDtypeStruct((rows, n), F32),
        compiler_params=_cparams(("arbitrary",)),
        name="modulation",
    )(c_pad, w_mod, b_mod)


_Q_TILES = Q_W // PROJ_TN
_KV_TILES = (2 * KV_W) // PROJ_TN
_U_TILES = POOL_W // PROJ_TN
_G_TILES = (2 * D_MODEL) // PROJ_TN
_KV_T0 = _Q_TILES
_U_T0 = _KV_T0 + _KV_TILES
_G_T0 = _U_T0 + _U_TILES
_PROJ_TILES = _G_T0 + _G_TILES


def _proj_kernel(x_ref, mod_ref, g_ref, w_ref, b_ref, q_ref, kv_ref, u_ref, gl_ref, h_ref):
    j = pl.program_id(1)

    @pl.when(j == 0)
    def _():
        m = mod_ref[0]
        h = _rms(x_ref[...]) * g_ref[...] * (1.0 + m[1:2, :]) + m[0:1, :]
        h_ref[...] = h.astype(BF16)

    y = jnp.dot(h_ref[...], w_ref[...].astype(BF16), preferred_element_type=F32) + b_ref[...]

    @pl.when(j < _KV_T0)
    def _():
        q_ref[...] = y.astype(BF16)

    @pl.when((j >= _KV_T0) & (j < _U_T0))
    def _():
        kv_ref[...] = y.astype(BF16)

    @pl.when((j >= _U_T0) & (j < _G_T0))
    def _():
        u_ref[...] = y

    @pl.when(j >= _G_T0)
    def _():
        gl_ref[...] = y


def _in_projection(x2, mod6, g_pre, w_in, b_in, seq):
    t, d = x2.shape
    tiles_per_seq = seq // PROJ_TM

    def clampj(j, lo, n):
        return jnp.clip(j - lo, 0, n - 1)

    return pl.pallas_call(
        _proj_kernel,
        grid=(t // PROJ_TM, _PROJ_TILES),
        in_specs=[
            pl.BlockSpec((PROJ_TM, d), lambda i, j: (i, 0)),
            pl.BlockSpec((1, N_MOD, d), lambda i, j: (i // tiles_per_seq, 0, 0)),
            pl.BlockSpec((1, d), lambda i, j: (0, 0)),
            pl.BlockSpec((d, PROJ_TN), lambda i, j: (0, j)),
            pl.BlockSpec((1, PROJ_TN), lambda i, j: (0, j)),
        ],
        out_specs=[
            pl.BlockSpec((PROJ_TM, PROJ_TN), lambda i, j: (i, clampj(j, 0, _Q_TILES))),
            pl.BlockSpec((PROJ_TM, PROJ_TN), lambda i, j: (i, clampj(j, _KV_T0, _KV_TILES))),
            pl.BlockSpec((PROJ_TM, PROJ_TN), lambda i, j: (i, clampj(j, _U_T0, _U_TILES))),
            pl.BlockSpec((PROJ_TM, PROJ_TN), lambda i, j: (i, clampj(j, _G_T0, _G_TILES))),
        ],
        out_shape=[
            jax.ShapeDtypeStruct((t, Q_W), BF16),
            jax.ShapeDtypeStruct((t, 2 * KV_W), BF16),
            jax.ShapeDtypeStruct((t, POOL_W), F32),
            jax.ShapeDtypeStruct((t, 2 * D_MODEL), F32),
        ],
        scratch_shapes=[pltpu.VMEM((PROJ_TM, d), BF16)],
        compiler_params=_cparams(("arbitrary", "arbitrary")),
        name="in_projection",
    )(x2, mod6, g_pre, w_in, b_in)


def _attn_kernel(sink_ref, q_ref, kvp_ref, kvc_ref, o_ref, *, blocks_per_seq):
    i = pl.program_id(0)
    nq = ATT_TQ // WINDOW
    rows = GROUP * WINDOW
    r_iota = lax.broadcasted_iota(jnp.int32, (rows, 2 * WINDOW), 0)
    k_iota = lax.broadcasted_iota(jnp.int32, (rows, 2 * WINDOW), 1)
    qi = r_iota % WINDOW
    band = (k_iota > qi) & (k_iota <= qi + WINDOW)
    g_of_row = lax.broadcasted_iota(jnp.int32, (rows, 1), 0) // WINDOW
    scale = HEAD_DIM ** -0.5

    for n in range(nq):
        if n == 0:
            first = (i % blocks_per_seq) == 0
            mask = band & ((k_iota >= WINDOW) | jnp.logical_not(first))
            kv_prev = kvp_ref[...]
        else:
            mask = band
            kv_prev = kvc_ref[(n - 1) * WINDOW:n * WINDOW, :]
        kv_cur = kvc_ref[n * WINDOW:(n + 1) * WINDOW, :]
        kv = jnp.concatenate([kv_prev, kv_cur], axis=0)
        qn = q_ref[n * WINDOW:(n + 1) * WINDOW, :]
        for h in range(N_KV_HEADS):
            k = kv[:, h * HEAD_DIM:(h + 1) * HEAD_DIM]
            v = kv[:, KV_W + h * HEAD_DIM:KV_W + (h + 1) * HEAD_DIM]
            qs = jnp.concatenate(
                [qn[:, (h * GROUP + g) * HEAD_DIM:(h * GROUP + g + 1) * HEAD_DIM]
                 for g in range(GROUP)], axis=0)
            s = lax.dot_general(qs, k, (((1,), (1,)), ((), ())),
                                preferred_element_type=F32) * scale
            s = jnp.where(mask, s, NEG_INF)
            sink = jnp.zeros((rows, 1), F32)
            for g in range(GROUP):
                sink = jnp.where(g_of_row == g, sink_ref[h * GROUP + g], sink)
            m = jnp.maximum(jnp.max(s, axis=-1, keepdims=True), sink)
            p = jnp.exp(s - m)
            denom = jnp.sum(p, axis=-1, keepdims=True) + jnp.exp(sink - m)
            o = jnp.dot(p.astype(BF16), v, preferred_element_type=F32) / denom
            for g in range(GROUP):
                c0 = (h * GROUP + g) * HEAD_DIM
                o_ref[n * WINDOW:(n + 1) * WINDOW, c0:c0 + HEAD_DIM] = (
                    o[g * WINDOW:(g + 1) * WINDOW, :].astype(BF16))


def _attention(q, kv, sinks, seq):
    t = q.shape[0]
    blocks_per_seq = seq // ATT_TQ
    wpb = ATT_TQ // WINDOW
    return pl.pallas_call(
        functools.partial(_attn_kernel, blocks_per_seq=blocks_per_seq),
        grid=(t // ATT_TQ,),
        in_specs=[
            pl.BlockSpec(memory_space=pltpu.SMEM),
            pl.BlockSpec((ATT_TQ, Q_W), lambda i: (i, 0)),
            pl.BlockSpec((WINDOW, 2 * KV_W), lambda i: (jnp.maximum(i * wpb - 1, 0), 0)),
            pl.BlockSpec((ATT_TQ, 2 * KV_W), lambda i: (i, 0)),
        ],
        out_specs=pl.BlockSpec((ATT_TQ, Q_W), lambda i: (i, 0)),
        out_shape=jax.ShapeDtypeStruct((t, Q_W), BF16),
        compiler_params=_cparams(("arbitrary",)),
        name="swa_attention",
    )(sinks, q, kv, kv)


def _mix_kernel(x_ref, a_ref, up_ref, u_ref, gl_ref, mod_ref, wpm_ref, ps_ref, wa_ref, wp_ref,
                wo_ref, gpost_ref, gffn_ref, wr_ref, br_ref,
                x1_ref, h2_ref, idx_ref, gw_ref, ubuf_ref, *, tiles_per_seq):
    i = pl.program_id(0)
    tm = MIX_TM
    seq_tile = i % tiles_per_seq
    m = mod_ref[0]

    halo = jnp.where(seq_tile == 0, 0.0, up_ref[...])
    ubuf_ref[0:POOL_HALO, :] = halo
    ubuf_ref[POOL_HALO:, :] = u_ref[...]
    pos = seq_tile * tm + lax.broadcasted_iota(jnp.int32, (tm, 1), 0)
    p_parts = []
    for g, w in enumerate(POOL_WINDOWS):
        c0 = g * POOL_GC
        acc = ubuf_ref[POOL_HALO:POOL_HALO + tm, c0:c0 + POOL_GC]
        ug = acc
        for dlt in range(1, w):
            acc = acc + ubuf_ref[POOL_HALO - dlt:POOL_HALO - dlt + tm, c0:c0 + POOL_GC]
        cnt = jnp.minimum(pos + 1, w).astype(F32)
        pg = acc / cnt - ug
        p_parts.append(jnp.dot(pg.astype(BF16), wpm_ref[g], preferred_element_type=F32))
    p = jnp.concatenate(p_parts, axis=-1) * ps_ref[...]

    ya = jnp.dot(a_ref[...], wa_ref[...], preferred_element_type=F32)
    yp = jnp.dot(p.astype(BF16), wp_ref[...], preferred_element_type=F32)
    gates_a = jax.nn.sigmoid(gl_ref[:, 0:D_MODEL])
    gates_p = jax.nn.sigmoid(gl_ref[:, D_MODEL:2 * D_MODEL])
    merged = gates_a * ya + gates_p * yp
    mix = jnp.dot(merged.astype(BF16), wo_ref[...], preferred_element_type=F32)
    x1 = x_ref[...] + m[2:3, :] * (_rms(mix) * gpost_ref[...])
    x1_ref[...] = x1

    h2 = _rms(x1) * gffn_ref[...] * (1.0 + m[4:5, :]) + m[3:4, :]
    h2_ref[...] = h2

    h_hi = h2.astype(BF16)
    h_lo = (h2 - h_hi.astype(F32)).astype(BF16)
    prod = jnp.dot(jnp.concatenate([h_hi, h_lo], axis=0), wr_ref[...],
                   preferred_element_type=F32)
    logits = ((prod[0:tm, 0:N_EXPERTS] + prod[0:tm, N_EXPERTS:])
              + (prod[tm:, 0:N_EXPERTS] + prod[tm:, N_EXPERTS:])) + br_ref[...]
    e_iota = lax.broadcasted_iota(jnp.int32, (tm, N_EXPERTS), 1)
    k_iota = lax.broadcasted_iota(jnp.int32, (tm, TOP_K), 1)
    vals = jnp.zeros((tm, TOP_K), F32)
    idxs = jnp.zeros((tm, TOP_K), jnp.int32)
    lg = logits
    for k in range(TOP_K):
        mx = jnp.max(lg, axis=-1, keepdims=True)
        ix = jnp.min(jnp.where(lg == mx, e_iota, N_EXPERTS), axis=-1, keepdims=True)
        vals = jnp.where(k_iota == k, mx, vals)
        idxs = jnp.where(k_iota == k, ix, idxs)
        lg = jnp.where(e_iota == ix, -jnp.inf, lg)
    ex = jnp.exp(vals - vals[:, 0:1])
    gw_ref[...] = ex / jnp.sum(ex, axis=-1, keepdims=True)
    idx_ref[...] = idxs


def _split_bf16(w):
    hi = w.astype(BF16)
    lo = (w - hi.astype(F32)).astype(BF16)
    return jnp.concatenate([hi, lo], axis=-1)


def _mix(x2, a, u, gl, mod6, wpm, ps, wa, wp, wo, gpost, gffn, wr, br, seq):
    t, d = x2.shape
    tm = MIX_TM
    tiles_per_seq = seq // tm
    hb = tm // POOL_HALO
    const2 = lambda i: (0, 0)
    return pl.pallas_call(
        functools.partial(_mix_kernel, tiles_per_seq=tiles_per_seq),
        grid=(t // tm,),
        in_specs=[
            pl.BlockSpec((tm, d), lambda i: (i, 0)),
            pl.BlockSpec((tm, Q_W), lambda i: (i, 0)),
            pl.BlockSpec((POOL_HALO, POOL_W), lambda i: (jnp.maximum(i * hb - 1, 0), 0)),
            pl.BlockSpec((tm, POOL_W), lambda i: (i, 0)),
            pl.BlockSpec((tm, 2 * d), lambda i: (i, 0)),
            pl.BlockSpec((1, N_MOD, d), lambda i: (i // tiles_per_seq, 0, 0)),
            pl.BlockSpec(wpm.shape, lambda i: (0, 0, 0), pipeline_mode=pl.Buffered(1)),
            pl.BlockSpec((1, POOL_W), const2),
            pl.BlockSpec(wa.shape, const2, pipeline_mode=pl.Buffered(1)),
            pl.BlockSpec(wp.shape, const2, pipeline_mode=pl.Buffered(1)),
            pl.BlockSpec(wo.shape, const2, pipeline_mode=pl.Buffered(1)),
            pl.BlockSpec((1, d), const2),
            pl.BlockSpec((1, d), const2),
            pl.BlockSpec(wr.shape, const2),
            pl.BlockSpec((1, N_EXPERTS), const2),
        ],
        out_specs=[
            pl.BlockSpec((tm, d), lambda i: (i, 0)),
            pl.BlockSpec((tm, d), lambda i: (i, 0)),
            pl.BlockSpec((tm, TOP_K), lambda i: (i, 0)),
            pl.BlockSpec((tm, TOP_K), lambda i: (i, 0)),
        ],
        out_shape=[
            jax.ShapeDtypeStruct((t, d), F32),
            jax.ShapeDtypeStruct((t, d), F32),
            jax.ShapeDtypeStruct((t, TOP_K), jnp.int32),
            jax.ShapeDtypeStruct((t, TOP_K), F32),
        ],
        scratch_shapes=[pltpu.VMEM((POOL_HALO + tm, POOL_W), F32)],
        compiler_params=_cparams(("arbitrary",)),
        name="mix_merge_router",
    )(x2, a, u, u, gl, mod6, wpm, ps, wa, wp, wo, gpost, gffn, wr, br)


def _dispatch_kernel(dest_ref, h_ref, xs_hbm, sem):
    n = DISP_TM * TOP_K

    def start(j, c):
        tok = lax.shift_right_logical(j, _LOG2_TOP_K)
        pltpu.make_async_copy(h_ref.at[pl.ds(tok, 1)],
                              xs_hbm.at[pl.ds(dest_ref[0, 0, j], 1)], sem).start()
        return c

    lax.fori_loop(0, n, start, 0, unroll=DMA_UNROLL)
    pltpu.make_async_copy(xs_hbm.at[pl.ds(0, n)], xs_hbm.at[pl.ds(0, n)], sem).wait()


def _dispatch(dest, h2, n_rows):
    t, d = h2.shape
    steps = t // DISP_TM
    dest3 = dest.reshape(steps, 1, DISP_TM * TOP_K)
    return pl.pallas_call(
        _dispatch_kernel,
        grid=(steps,),
        in_specs=[
            pl.BlockSpec((1, 1, DISP_TM * TOP_K), lambda i: (i, 0, 0), memory_space=pltpu.SMEM),
            pl.BlockSpec((DISP_TM, d), lambda i: (i, 0)),
        ],
        out_specs=pl.BlockSpec(memory_space=pl.ANY),
        out_shape=jax.ShapeDtypeStruct((n_rows, d), F32),
        scratch_shapes=[pltpu.SemaphoreType.DMA(())],
        compiler_params=_cparams(("arbitrary",)),
        name="moe_dispatch",
    )(dest3, h2)


_MOE_NC = D_FF // MOE_FC
_GU_W = 2 * MOE_FC


def _moe_kernel(ie_ref, ir_ref, ins_ref, inv_ref, xs_hbm, w1_ref, b1_ref, w2_ref, b2_ref, y_hbm,
                xs_v, acc_v, w1_v, w2f_v, w2_v, gu_v, in_sem, out_sem):
    i = pl.program_id(0)

    @pl.when(ins_ref[i] > 0)
    def _():
        _moe_item(ir_ref[i], ins_ref[i], inv_ref[i], xs_hbm, w1_ref, b1_ref, w2_ref, b2_ref,
                  y_hbm, xs_v, acc_v, w1_v, w2f_v, w2_v, gu_v, in_sem, out_sem)


def _moe_item(row0, nsub, nvalid, xs_hbm, w1_ref, b1_ref, w2_ref, b2_ref, y_hbm,
              xs_v, acc_v, w1_v, w2f_v, w2_v, gu_v, in_sem, out_sem):
    c = pl.program_id(1)
    r = MOE_R

    def rows(s):
        return pl.ds(pl.multiple_of(s * r, r), r)

    def in_copy(s):
        return pltpu.make_async_copy(
            xs_hbm.at[pl.ds(pl.multiple_of(row0 + s * r, r), r)], acc_v.at[rows(s)],
            in_sem.at[s])

    def out_copy(s):
        return pltpu.make_async_copy(
            acc_v.at[rows(s)], y_hbm.at[pl.ds(pl.multiple_of(row0 + s * r, r), r)], out_sem)

    def each(fn):
        def body(s, carry):
            fn(s)
            return carry
        lax.fori_loop(0, nsub, body, 0)

    w1_v[...] = w1_ref[0].astype(BF16)
    half = LANES // 2
    for cb in range(D_MODEL // LANES):
        cs = slice(cb * LANES, (cb + 1) * LANES)
        for blk in range(MOE_FC // LANES):
            b0 = blk * LANES
            w2f_v[cb, pl.ds(b0, half, stride=2), :] = w2_ref[0, b0:b0 + half, cs]
            w2f_v[cb, pl.ds(b0 + 1, half, stride=2), :] = w2_ref[0, b0 + half:b0 + LANES, cs]
        w2_v[:, cs] = w2f_v[cb].astype(BF16)

    lane = lax.broadcasted_iota(jnp.int32, (r, LANES), 1)
    even = (lane % 2) == 0
    b1 = b1_ref[0]

    def gate_up(s):
        return jnp.dot(xs_v[rows(s), :], w1_v[...], preferred_element_type=F32) + b1

    def activation(gu):
        glu = jnp.minimum(gu, SWIGLU_LIMIT)
        f = glu * jax.nn.sigmoid(SWIGLU_ALPHA * glu)
        lin = jnp.clip(gu, -SWIGLU_LIMIT, SWIGLU_LIMIT) + 1.0
        parts = []
        for blk in range(MOE_FC // LANES):
            ca = 2 * blk * LANES
            cb = ca + LANES
            za = f[:, ca:ca + LANES] * pltpu.roll(lin[:, ca:ca + LANES], LANES - 1, 1)
            zb = pltpu.roll(f[:, cb:cb + LANES], 1, 1) * lin[:, cb:cb + LANES]
            parts.append(jnp.where(even, za, zb))
        return jnp.concatenate(parts, axis=-1).astype(BF16)

    def run(first, last):
        def load_rows(s):
            in_copy(s).wait()
            live = (s * r + lax.broadcasted_iota(jnp.int32, (r, 1), 0)) < nvalid
            xs_v[rows(s), :] = jnp.where(live, acc_v[rows(s), :], 0.0).astype(BF16)

        def finish(s, gu):
            contrib = jnp.dot(activation(gu), w2_v[...], preferred_element_type=F32)
            if first:
                acc_v[rows(s), :] = contrib
            elif last:
                acc_v[rows(s), :] = acc_v[rows(s), :] + contrib + b2_ref[0]
                out_copy(s).start()
            else:
                acc_v[rows(s), :] = acc_v[rows(s), :] + contrib

        if first:
            each(lambda s: in_copy(s).start())
            load_rows(0)
        gu_v[0] = gate_up(0)

        def body(s, carry):
            slot = s & 1
            if first:
                load_rows(s + 1)
            nxt = gate_up(s + 1)
            finish(s, gu_v[slot])
            gu_v[1 - slot] = nxt
            return carry

        lax.fori_loop(0, nsub - 1, body, 0)
        finish(nsub - 1, gu_v[(nsub - 1) & 1])
        if last:
            each(lambda s: out_copy(s).wait())

    @pl.when(c == 0)
    def _():
        run(True, False)

    @pl.when((c > 0) & (c < _MOE_NC - 1))
    def _():
        run(False, False)

    @pl.when(c == _MOE_NC - 1)
    def _():
        run(False, True)


def _moe(item_e, item_row0, item_nsub, item_nvalid, xs, w_gate_up, b_gate_up, w_down, b_down):
    n_items = item_e.shape[0]
    n_rows, d = xs.shape
    e = w_gate_up.shape[0]
    b1 = b_gate_up.reshape(e, 1, 2 * D_FF)
    b2 = b_down.reshape(e, 1, d)

    def chunk(i, c, ins):
        return jnp.where(ins[i] > 0, c, _MOE_NC - 1)

    grid_spec = pltpu.PrefetchScalarGridSpec(
        num_scalar_prefetch=4,
        grid=(n_items, _MOE_NC),
        in_specs=[
            pl.BlockSpec(memory_space=pl.ANY),
            pl.BlockSpec((1, d, _GU_W), lambda i, c, ie, ir, ins, inv: (ie[i], 0, chunk(i, c, ins))),
            pl.BlockSpec((1, 1, _GU_W), lambda i, c, ie, ir, ins, inv: (ie[i], 0, chunk(i, c, ins))),
            pl.BlockSpec((1, MOE_FC, d), lambda i, c, ie, ir, ins, inv: (ie[i], chunk(i, c, ins), 0)),
            pl.BlockSpec((1, 1, d), lambda i, c, ie, ir, ins, inv: (ie[i], 0, 0)),
        ],
        out_specs=pl.BlockSpec(memory_space=pl.ANY),
        scratch_shapes=[
            pltpu.VMEM((MOE_TMAX, d), BF16),
            pltpu.VMEM((MOE_TMAX, d), F32),
            pltpu.VMEM((d, _GU_W), BF16),
            pltpu.VMEM((d // LANES, MOE_FC, LANES), F32),
            pltpu.VMEM((MOE_FC, d), BF16),
            pltpu.VMEM((2, MOE_R, _GU_W), F32),
            pltpu.SemaphoreType.DMA((MOE_TMAX // MOE_R,)),
            pltpu.SemaphoreType.DMA(()),
        ],
    )
    return pl.pallas_call(
        _moe_kernel,
        grid_spec=grid_spec,
        out_shape=jax.ShapeDtypeStruct((n_rows, d), F32),
        compiler_params=_cparams(("arbitrary", "arbitrary")),
        name="moe_experts",
    )(item_e, item_row0, item_nsub, item_nvalid, xs, w_gate_up, b1, w_down, b2)


def _combine_kernel(dest_ref, y_hbm, gw_ref, x1_ref, mod_ref, g_ref, o_ref, buf, sem):
    n = COMB_TM * TOP_K

    def start(j, c):
        pltpu.make_async_copy(
            y_hbm.at[pl.ds(dest_ref[0, 0, j], 1)],
            buf.at[j & (TOP_K - 1), pl.ds(lax.shift_right_logical(j, _LOG2_TOP_K), 1)],
            sem).start()
        return c

    lax.fori_loop(0, n, start, 0, unroll=DMA_UNROLL)
    pltpu.make_async_copy(buf, buf, sem).wait()

    gw = gw_ref[...]
    y = buf[0] * gw[:, 0:1]
    for k in range(1, TOP_K):
        y = y + buf[k] * gw[:, k:k + 1]
    m = mod_ref[0]
    o_ref[...] = x1_ref[...] + m[5:6, :] * (_rms(y) * g_ref[...])


def _combine(dest, yb, gw, x1, mod6, g_post, seq):
    t, d = x1.shape
    steps = t // COMB_TM
    tiles_per_seq = seq // COMB_TM
    dest3 = dest.reshape(steps, 1, COMB_TM * TOP_K)
    return pl.pallas_call(
        _combine_kernel,
        grid=(steps,),
        in_specs=[
            pl.BlockSpec((1, 1, COMB_TM * TOP_K), lambda i: (i, 0, 0), memory_space=pltpu.SMEM),
            pl.BlockSpec(memory_space=pl.ANY),
            pl.BlockSpec((COMB_TM, TOP_K), lambda i: (i, 0)),
            pl.BlockSpec((COMB_TM, d), lambda i: (i, 0)),
            pl.BlockSpec((1, N_MOD, d), lambda i: (i // tiles_per_seq, 0, 0)),
            pl.BlockSpec((1, d), lambda i: (0, 0)),
        ],
        out_specs=pl.BlockSpec((COMB_TM, d), lambda i: (i, 0)),
        out_shape=jax.ShapeDtypeStruct((t, d), F32),
        scratch_shapes=[pltpu.VMEM((TOP_K, COMB_TM, d), F32), pltpu.SemaphoreType.DMA(())],
        compiler_params=_cparams(("arbitrary",)),
        name="moe_combine",
    )(dest3, yb, gw, x1, mod6, g_post)


def _routing_tables(top_idx, n_items):
    t = top_idx.shape[0]
    sel = jnp.sum((top_idx[:, :, None] == jnp.arange(N_EXPERTS, dtype=jnp.int32)[None, None, :])
                  .astype(jnp.int32), axis=1)
    incl = jnp.cumsum(sel, axis=0)
    rank = incl - sel
    counts = incl[-1]
    pcounts = (counts + MOE_R - 1) // MOE_R * MOE_R
    pend = jnp.cumsum(pcounts)
    pstart = pend - pcounts
    dest = pstart[top_idx] + jnp.take_along_axis(rank, top_idx, axis=1)

    tiles = (pcounts + MOE_TMAX - 1) // MOE_TMAX
    tend = jnp.cumsum(tiles)
    tstart = tend - tiles
    it = jnp.arange(n_items, dtype=jnp.int32)
    ie = jnp.minimum(jnp.sum((tend[None, :] <= it[:, None]).astype(jnp.int32), axis=1),
                     N_EXPERTS - 1)
    valid = it < tend[-1]
    local = it - tstart[ie]
    row0 = pstart[ie] + local * MOE_TMAX
    nrows = jnp.clip(pcounts[ie] - local * MOE_TMAX, 0, MOE_TMAX)
    nsub = jnp.where(valid, nrows // MOE_R, 0).astype(jnp.int32)
    nvalid = jnp.where(valid, jnp.clip(counts[ie] - local * MOE_TMAX, 0, MOE_TMAX), 0)
    row0 = jnp.where(valid, row0, 0).astype(jnp.int32)
    return dest.astype(jnp.int32), ie, row0, nsub, nvalid.astype(jnp.int32)


def kernel(x, c, w_mod, b_mod, g_pre_mix, g_post_mix, w_in, b_in, attn_sinks, w_pool_mix,
           pool_scale, w_attn_branch, w_pool_branch, w_out, g_pre_ffn, g_post_ffn, w_router,
           b_router, w_gate_up, b_gate_up, w_down, b_down):
    b, s, d = x.shape
    t = b * s
    depth = w_mod.shape[0]
    n_assign = t * TOP_K
    n_items = N_EXPERTS + n_assign // MOE_TMAX
    n_rows = n_assign + N_EXPERTS * MOE_R

    x2 = x.reshape(t, d)
    c_pad = jnp.pad(c, ((0, SUBLANES - b), (0, 0)))
    for l in range(depth):
        mod = _modulation(c_pad, w_mod[l], b_mod[l][None, :])
        mod6 = mod[:b].reshape(b, N_MOD, d)

        q, kv, u, gl = _in_projection(x2, mod6, g_pre_mix[l][None, :], w_in[l],
                                      b_in[l][None, :], s)
        a = _attention(q, kv, attn_sinks[l], s)
        x1, h2, top_idx, gate_w = _mix(
            x2, a, u, gl, mod6, w_pool_mix[l].astype(BF16), pool_scale[l][None, :],
            w_attn_branch[l].astype(BF16), w_pool_branch[l].astype(BF16),
            w_out[l].astype(BF16), g_post_mix[l][None, :], g_pre_ffn[l][None, :],
            _split_bf16(w_router[l]), b_router[l][None, :], s)

        dest, item_e, item_row0, item_nsub, item_nvalid = _routing_tables(top_idx, n_items)
        xs = _dispatch(dest, h2, n_rows)
        yb = _moe(item_e, item_row0, item_nsub, item_nvalid, xs, w_gate_up[l], b_gate_up[l],
                  w_down[l], b_down[l])
        x2 = _combine(dest, yb, gate_w, x1, mod6, g_post_ffn[l][None, :], s)
    return x2.reshape(b, s, d)
```

```python
import functools

import jax
import jax.numpy as jnp
from jax import lax
from jax.experimental import pallas as pl
from jax.experimental.pallas import tpu as pltpu

F32 = jnp.float32
BF16 = jnp.bfloat16

D_MODEL = 2048
HEAD_DIM = 64
N_Q_HEADS = 16
N_KV_HEADS = 4
GROUP = N_Q_HEADS // N_KV_HEADS
WINDOW = 128
Q_W = N_Q_HEADS * HEAD_DIM
KV_W = N_KV_HEADS * HEAD_DIM
POOL_WINDOWS = (2, 4, 8, 16)
POOL_W = D_MODEL // 2
POOL_GC = POOL_W // len(POOL_WINDOWS)
POOL_HALO = 16
N_EXPERTS = 32
TOP_K = 4
_LOG2_TOP_K = 2
D_FF = D_MODEL
SWIGLU_LIMIT = 7.0
SWIGLU_ALPHA = 1.702
N_MOD = 6
RMS_EPS = 1e-6
NEG_INF = -1e30

LANES = 128
SUBLANES = 8
VMEM_LIMIT_BYTES = 56 * 1024 * 1024

MOD_TN = 1024
PROJ_TM = 1024
PROJ_TN = 512
ATT_TQ = 512
MIX_TM = 256
MOE_R = 256
MOE_TMAX = 1536
MOE_FC = 256
DISP_TM = 256
COMB_TM = 128
DMA_UNROLL = 8


def _cparams(sem):
    return pltpu.CompilerParams(dimension_semantics=sem, vmem_limit_bytes=VMEM_LIMIT_BYTES)


def _rms(x):
    return x * lax.rsqrt(jnp.mean(x * x, axis=-1, keepdims=True) + RMS_EPS)


def _mod_kernel(c_ref, w_ref, b_ref, o_ref):
    c = c_ref[...]
    s = c * jax.nn.sigmoid(c)
    o_ref[...] = jnp.dot(s.astype(BF16), w_ref[...].astype(BF16),
                         preferred_element_type=F32) + b_ref[...]


def _modulation(c_pad, w_mod, b_mod):
    rows, d = c_pad.shape
    n = w_mod.shape[1]
    return pl.pallas_call(
        _mod_kernel,
        grid=(n // MOD_TN,),
        in_specs=[
            pl.BlockSpec((rows, d), lambda j: (0, 0)),
            pl.BlockSpec((d, MOD_TN), lambda j: (0, j)),
            pl.BlockSpec((1, MOD_TN), lambda j: (0, j)),
        ],
        out_specs=pl.BlockSpec((rows, MOD_TN), lambda j: (0, j)),
        out_shape=jax.ShapeDtypeStruct((rows, n), F32),
        compiler_params=_cparams(("arbitrary",)),
        name="modulation",
    )(c_pad, w_mod, b_mod)


_Q_TILES = Q_W // PROJ_TN
_KV_TILES = (2 * KV_W) // PROJ_TN
_U_TILES = POOL_W // PROJ_TN
_G_TILES = (2 * D_MODEL) // PROJ_TN
_KV_T0 = _Q_TILES
_U_T0 = _KV_T0 + _KV_TILES
_G_T0 = _U_T0 + _U_TILES
_PROJ_TILES = _G_T0 + _G_TILES


def _proj_kernel(x_ref, mod_ref, g_ref, w_ref, b_ref, q_ref, kv_ref, u_ref, gl_ref, h_ref):
    j = pl.program_id(1)

    @pl.when(j == 0)
    def _():
        m = mod_ref[0]
        h = _rms(x_ref[...]) * g_ref[...] * (1.0 + m[1:2, :]) + m[0:1, :]
        h_ref[...] = h.astype(BF16)

    y = jnp.dot(h_ref[...], w_ref[...], preferred_element_type=F32) + b_ref[...]

    @pl.when(j < _KV_T0)
    def _():
        q_ref[...] = y.astype(BF16)

    @pl.when((j >= _KV_T0) & (j < _U_T0))
    def _():
        kv_ref[...] = y.astype(BF16)

    @pl.when((j >= _U_T0) & (j < _G_T0))
    def _():
        u_ref[...] = y

    @pl.when(j >= _G_T0)
    def _():
        gl_ref[...] = y


def _in_projection(x2, mod6, g_pre, w_in, b_in, seq):
    t, d = x2.shape
    tiles_per_seq = seq // PROJ_TM

    def clampj(j, lo, n):
        return jnp.clip(j - lo, 0, n - 1)

    return pl.pallas_call(
        _proj_kernel,
        grid=(t // PROJ_TM, _PROJ_TILES),
        in_specs=[
            pl.BlockSpec((PROJ_TM, d), lambda i, j: (i, 0)),
            pl.BlockSpec((1, N_MOD, d), lambda i, j: (i // tiles_per_seq, 0, 0)),
            pl.BlockSpec((1, d), lambda i, j: (0, 0)),
            pl.BlockSpec((d, PROJ_TN), lambda i, j: (0, j)),
            pl.BlockSpec((1, PROJ_TN), lambda i, j: (0, j)),
        ],
        out_specs=[
            pl.BlockSpec((PROJ_TM, PROJ_TN), lambda i, j: (i, clampj(j, 0, _Q_TILES))),
            pl.BlockSpec((PROJ_TM, PROJ_TN), lambda i, j: (i, clampj(j, _KV_T0, _KV_TILES))),
            pl.BlockSpec((PROJ_TM, PROJ_TN), lambda i, j: (i, clampj(j, _U_T0, _U_TILES))),
            pl.BlockSpec((PROJ_TM, PROJ_TN), lambda i, j: (i, clampj(j, _G_T0, _G_TILES))),
        ],
        out_shape=[
            jax.ShapeDtypeStruct((t, Q_W), BF16),
            jax.ShapeDtypeStruct((t, 2 * KV_W), BF16),
            jax.ShapeDtypeStruct((t, POOL_W), F32),
            jax.ShapeDtypeStruct((t, 2 * D_MODEL), F32),
        ],
        scratch_shapes=[pltpu.VMEM((PROJ_TM, d), BF16)],
        compiler_params=_cparams(("arbitrary", "arbitrary")),
        name="in_projection",
    )(x2, mod6, g_pre, w_in, b_in)


def _attn_kernel(sink_ref, q_ref, kvp_ref, kvc_ref, o_ref, *, blocks_per_seq):
    i = pl.program_id(0)
    nq = ATT_TQ // WINDOW
    rows = GROUP * WINDOW
    r_iota = lax.broadcasted_iota(jnp.int32, (rows, 2 * WINDOW), 0)
    k_iota = lax.broadcasted_iota(jnp.int32, (rows, 2 * WINDOW), 1)
    qi = r_iota % WINDOW
    band = (k_iota > qi) & (k_iota <= qi + WINDOW)
    g_of_row = lax.broadcasted_iota(jnp.int32, (rows, 1), 0) // WINDOW
    scale = HEAD_DIM ** -0.5

    for n in range(nq):
        if n == 0:
            first = (i % blocks_per_seq) == 0
            mask = band & ((k_iota >= WINDOW) | jnp.logical_not(first))
            kv_prev = kvp_ref[...]
        else:
            mask = band
            kv_prev = kvc_ref[(n - 1) * WINDOW:n * WINDOW, :]
        kv_cur = kvc_ref[n * WINDOW:(n + 1) * WINDOW, :]
        kv = jnp.concatenate([kv_prev, kv_cur], axis=0)
        qn = q_ref[n * WINDOW:(n + 1) * WINDOW, :]
        for h in range(N_KV_HEADS):
            k = kv[:, h * HEAD_DIM:(h + 1) * HEAD_DIM]
            v = kv[:, KV_W + h * HEAD_DIM:KV_W + (h + 1) * HEAD_DIM]
            qs = jnp.concatenate(
                [qn[:, (h * GROUP + g) * HEAD_DIM:(h * GROUP + g + 1) * HEAD_DIM]
                 for g in range(GROUP)], axis=0)
            s = lax.dot_general(qs, k, (((1,), (1,)), ((), ())),
                                preferred_element_type=F32) * scale
            s = jnp.where(mask, s, NEG_INF)
            sink = jnp.zeros((rows, 1), F32)
            for g in range(GROUP):
                sink = jnp.where(g_of_row == g, sink_ref[h * GROUP + g], sink)
            m = jnp.maximum(jnp.max(s, axis=-1, keepdims=True), sink)
            p = jnp.exp(s - m)
            denom = jnp.sum(p, axis=-1, keepdims=True) + jnp.exp(sink - m)
            o = jnp.dot(p.astype(BF16), v, preferred_element_type=F32) / denom
            for g in range(GROUP):
                c0 = (h * GROUP + g) * HEAD_DIM
                o_ref[n * WINDOW:(n + 1) * WINDOW, c0:c0 + HEAD_DIM] = (
                    o[g * WINDOW:(g + 1) * WINDOW, :].astype(BF16))


def _attention(q, kv, sinks, seq):
    t = q.shape[0]
    blocks_per_seq = seq // ATT_TQ
    wpb = ATT_TQ // WINDOW
    return pl.pallas_call(
        functools.partial(_attn_kernel, blocks_per_seq=blocks_per_seq),
        grid=(t // ATT_TQ,),
        in_specs=[
            pl.BlockSpec(memory_space=pltpu.SMEM),
            pl.BlockSpec((ATT_TQ, Q_W), lambda i: (i, 0)),
            pl.BlockSpec((WINDOW, 2 * KV_W), lambda i: (jnp.maximum(i * wpb - 1, 0), 0)),
            pl.BlockSpec((ATT_TQ, 2 * KV_W), lambda i: (i, 0)),
        ],
        out_specs=pl.BlockSpec((ATT_TQ, Q_W), lambda i: (i, 0)),
        out_shape=jax.ShapeDtypeStruct((t, Q_W), BF16),
        compiler_params=_cparams(("arbitrary",)),
        name="swa_attention",
    )(sinks, q, kv, kv)


def _mix_kernel(x_ref, a_ref, up_ref, u_ref, gl_ref, mod_ref, wpm_ref, ps_ref, wa_ref, wp_ref,
                wo_ref, gpost_ref, gffn_ref, wr_ref, br_ref,
                x1_ref, h2_ref, idx_ref, gw_ref, ubuf_ref, *, tiles_per_seq):
    i = pl.program_id(0)
    tm = MIX_TM
    seq_tile = i % tiles_per_seq
    m = mod_ref[0]

    halo = jnp.where(seq_tile == 0, 0.0, up_ref[...])
    ubuf_ref[0:POOL_HALO, :] = halo
    ubuf_ref[POOL_HALO:, :] = u_ref[...]
    pos = seq_tile * tm + lax.broadcasted_iota(jnp.int32, (tm, 1), 0)
    p_parts = []
    for g, w in enumerate(POOL_WINDOWS):
        c0 = g * POOL_GC
        acc = ubuf_ref[POOL_HALO:POOL_HALO + tm, c0:c0 + POOL_GC]
        ug = acc
        for dlt in range(1, w):
            acc = acc + ubuf_ref[POOL_HALO - dlt:POOL_HALO - dlt + tm, c0:c0 + POOL_GC]
        cnt = jnp.minimum(pos + 1, w).astype(F32)
        pg = acc / cnt - ug
        p_parts.append(jnp.dot(pg.astype(BF16), wpm_ref[g], preferred_element_type=F32))
    p = jnp.concatenate(p_parts, axis=-1) * ps_ref[...]

    ya = jnp.dot(a_ref[...], wa_ref[...], preferred_element_type=F32)
    yp = jnp.dot(p.astype(BF16), wp_ref[...], preferred_element_type=F32)
    gates_a = jax.nn.sigmoid(gl_ref[:, 0:D_MODEL])
    gates_p = jax.nn.sigmoid(gl_ref[:, D_MODEL:2 * D_MODEL])
    merged = gates_a * ya + gates_p * yp
    mix = jnp.dot(merged.astype(BF16), wo_ref[...], preferred_element_type=F32)
    x1 = x_ref[...] + m[2:3, :] * (_rms(mix) * gpost_ref[...])
    x1_ref[...] = x1

    h2 = _rms(x1) * gffn_ref[...] * (1.0 + m[4:5, :]) + m[3:4, :]
    h2_ref[...] = h2

    h_hi = h2.astype(BF16)
    h_lo = (h2 - h_hi.astype(F32)).astype(BF16)
    prod = jnp.dot(jnp.concatenate([h_hi, h_lo], axis=0), wr_ref[...],
                   preferred_element_type=F32)
    logits = ((prod[0:tm, 0:N_EXPERTS] + prod[0:tm, N_EXPERTS:])
              + (prod[tm:, 0:N_EXPERTS] + prod[tm:, N_EXPERTS:])) + br_ref[...]
    e_iota = lax.broadcasted_iota(jnp.int32, (tm, N_EXPERTS), 1)
    k_iota = lax.broadcasted_iota(jnp.int32, (tm, TOP_K), 1)
    vals = jnp.zeros((tm, TOP_K), F32)
    idxs = jnp.zeros((tm, TOP_K), jnp.int32)
    lg = logits
    for k in range(TOP_K):
        mx = jnp.max(lg, axis=-1, keepdims=True)
        ix = jnp.min(jnp.where(lg == mx, e_iota, N_EXPERTS), axis=-1, keepdims=True)
        vals = jnp.where(k_iota == k, mx, vals)
        idxs = jnp.where(k_iota == k, ix, idxs)
        lg = jnp.where(e_iota == ix, -jnp.inf, lg)
    ex = jnp.exp(vals - vals[:, 0:1])
    gw_ref[...] = ex / jnp.sum(ex, axis=-1, keepdims=True)
    idx_ref[...] = idxs


def _split_bf16(w):
    hi = w.astype(BF16)
    lo = (w - hi.astype(F32)).astype(BF16)
    return jnp.concatenate([hi, lo], axis=-1)


def _mix(x2, a, u, gl, mod6, wpm, ps, wa, wp, wo, gpost, gffn, wr, br, seq):
    t, d = x2.shape
    tm = MIX_TM
    tiles_per_seq = seq // tm
    hb = tm // POOL_HALO
    const2 = lambda i: (0, 0)
    return pl.pallas_call(
        functools.partial(_mix_kernel, tiles_per_seq=tiles_per_seq),
        grid=(t // tm,),
        in_specs=[
            pl.BlockSpec((tm, d), lambda i: (i, 0)),
            pl.BlockSpec((tm, Q_W), lambda i: (i, 0)),
            pl.BlockSpec((POOL_HALO, POOL_W), lambda i: (jnp.maximum(i * hb - 1, 0), 0)),
            pl.BlockSpec((tm, POOL_W), lambda i: (i, 0)),
            pl.BlockSpec((tm, 2 * d), lambda i: (i, 0)),
            pl.BlockSpec((1, N_MOD, d), lambda i: (i // tiles_per_seq, 0, 0)),
            pl.BlockSpec(wpm.shape, lambda i: (0, 0, 0), pipeline_mode=pl.Buffered(1)),
            pl.BlockSpec((1, POOL_W), const2),
            pl.BlockSpec(wa.shape, const2, pipeline_mode=pl.Buffered(1)),
            pl.BlockSpec(wp.shape, const2, pipeline_mode=pl.Buffered(1)),
            pl.BlockSpec(wo.shape, const2, pipeline_mode=pl.Buffered(1)),
            pl.BlockSpec((1, d), const2),
            pl.BlockSpec((1, d), const2),
            pl.BlockSpec(wr.shape, const2),
            pl.BlockSpec((1, N_EXPERTS), const2),
        ],
        out_specs=[
            pl.BlockSpec((tm, d), lambda i: (i, 0)),
            pl.BlockSpec((tm, d), lambda i: (i, 0)),
            pl.BlockSpec((tm, TOP_K), lambda i: (i, 0)),
            pl.BlockSpec((tm, TOP_K), lambda i: (i, 0)),
        ],
        out_shape=[
            jax.ShapeDtypeStruct((t, d), F32),
            jax.ShapeDtypeStruct((t, d), F32),
            jax.ShapeDtypeStruct((t, TOP_K), jnp.int32),
            jax.ShapeDtypeStruct((t, TOP_K), F32),
        ],
        scratch_shapes=[pltpu.VMEM((POOL_HALO + tm, POOL_W), F32)],
        compiler_params=_cparams(("arbitrary",)),
        name="mix_merge_router",
    )(x2, a, u, u, gl, mod6, wpm, ps, wa, wp, wo, gpost, gffn, wr, br)


def _dispatch_kernel(dest_ref, h_ref, xs_hbm, sem):
    n = DISP_TM * TOP_K

    def start(j, c):
        tok = lax.shift_right_logical(j, _LOG2_TOP_K)
        pltpu.make_async_copy(h_ref.at[pl.ds(tok, 1)],
                              xs_hbm.at[pl.ds(dest_ref[0, 0, j], 1)], sem).start()
        return c

    lax.fori_loop(0, n, start, 0, unroll=DMA_UNROLL)
    pltpu.make_async_copy(xs_hbm.at[pl.ds(0, n)], xs_hbm.at[pl.ds(0, n)], sem).wait()


def _dispatch(dest, h2, n_rows):
    t, d = h2.shape
    steps = t // DISP_TM
    dest3 = dest.reshape(steps, 1, DISP_TM * TOP_K)
    return pl.pallas_call(
        _dispatch_kernel,
        grid=(steps,),
        in_specs=[
            pl.BlockSpec((1, 1, DISP_TM * TOP_K), lambda i: (i, 0, 0), memory_space=pltpu.SMEM),
            pl.BlockSpec((DISP_TM, d), lambda i: (i, 0)),
        ],
        out_specs=pl.BlockSpec(memory_space=pl.ANY),
        out_shape=jax.ShapeDtypeStruct((n_rows, d), F32),
        scratch_shapes=[pltpu.SemaphoreType.DMA(())],
        compiler_params=_cparams(("arbitrary",)),
        name="moe_dispatch",
    )(dest3, h2)


_MOE_NC = D_FF // MOE_FC
_GU_W = 2 * MOE_FC


def _moe_kernel(ie_ref, ir_ref, ins_ref, inv_ref, xs_hbm, w1_hbm, b1_ref, w2_hbm, b2_ref, y_hbm,
                xs_v, acc_v, stage_v, w1f_v, w2f_v, w1_v, w2i_v, w2_v, gu_v,
                in_sem, out_sem, w_sem):
    i = pl.program_id(0)

    @pl.when(ins_ref[i] > 0)
    def _():
        _moe_item(i, ie_ref, ir_ref, ins_ref, inv_ref, xs_hbm, w1_hbm, b1_ref, w2_hbm, b2_ref,
                  y_hbm, xs_v, acc_v, stage_v, w1f_v, w2f_v, w1_v, w2i_v, w2_v, gu_v,
                  in_sem, out_sem, w_sem)


def _moe_item(i, ie_ref, ir_ref, ins_ref, inv_ref, xs_hbm, w1_hbm, b1_ref, w2_hbm, b2_ref, y_hbm,
              xs_v, acc_v, stage_v, w1f_v, w2f_v, w1_v, w2i_v, w2_v, gu_v,
              in_sem, out_sem, w_sem):
    row0 = ir_ref[i]
    nsub = ins_ref[i]
    nvalid = inv_ref[i]
    has_next = ins_ref[i + 1] > 0
    r = MOE_R
    last_c = _MOE_NC - 1

    def rows(s):
        return pl.ds(pl.multiple_of(s * r, r), r)

    def in_copy(item, s):
        src0 = ir_ref[item] + s * r
        return pltpu.make_async_copy(
            xs_hbm.at[pl.ds(pl.multiple_of(src0, r), r)], stage_v.at[rows(s)], in_sem.at[s])

    def out_copy(s):
        return pltpu.make_async_copy(
            acc_v.at[rows(s)], y_hbm.at[pl.ds(pl.multiple_of(row0 + s * r, r), r)], out_sem)

    def weight_copies(item, c, slot):
        e = ie_ref[item]
        return (
            pltpu.make_async_copy(
                w1_hbm.at[e, :, pl.ds(pl.multiple_of(c * _GU_W, _GU_W), _GU_W)],
                w1f_v.at[slot], w_sem.at[0, slot]),
            pltpu.make_async_copy(
                w2_hbm.at[e, pl.ds(pl.multiple_of(c * MOE_FC, MOE_FC), MOE_FC), :],
                w2f_v.at[slot], w_sem.at[1, slot]),
        )

    def start_weights(item, c, slot):
        for cp in weight_copies(item, c, slot):
            cp.start()

    def wait_weights(c, slot):
        for cp in weight_copies(i, c, slot):
            cp.wait()

    def each(n, fn):
        def body(s, carry):
            fn(s)
            return carry
        lax.fori_loop(0, n, body, 0)

    def prep_weights(slot):
        w1_v[...] = w1f_v[slot].astype(BF16)
        half = LANES // 2
        for cb in range(D_MODEL // LANES):
            cs = slice(cb * LANES, (cb + 1) * LANES)
            for blk in range(MOE_FC // LANES):
                b0 = blk * LANES
                w2i_v[cb, pl.ds(b0, half, stride=2), :] = w2f_v[slot, b0:b0 + half, cs]
                w2i_v[cb, pl.ds(b0 + 1, half, stride=2), :] = w2f_v[slot, b0 + half:b0 + LANES, cs]
            w2_v[slot, :, cs] = w2i_v[cb].astype(BF16)

    lane = lax.broadcasted_iota(jnp.int32, (r, LANES), 1)
    even = (lane % 2) == 0

    def gate_up(s, c):
        b1 = b1_ref[0, :, pl.ds(pl.multiple_of(c * _GU_W, _GU_W), _GU_W)]
        return jnp.dot(xs_v[rows(s), :], w1_v[...], preferred_element_type=F32) + b1

    def activation(gu):
        glu = jnp.minimum(gu, SWIGLU_LIMIT)
        f = glu * jax.nn.sigmoid(SWIGLU_ALPHA * glu)
        lin = jnp.clip(gu, -SWIGLU_LIMIT, SWIGLU_LIMIT) + 1.0
        parts = []
        for blk in range(MOE_FC // LANES):
            ca = 2 * blk * LANES
            cb = ca + LANES
            za = f[:, ca:ca + LANES] * pltpu.roll(lin[:, ca:ca + LANES], LANES - 1, 1)
            zb = pltpu.roll(f[:, cb:cb + LANES], 1, 1) * lin[:, cb:cb + LANES]
            parts.append(jnp.where(even, za, zb))
        return jnp.concatenate(parts, axis=-1).astype(BF16)

    def load_rows(s):
        in_copy(i, s).wait()
        live = (s * r + lax.broadcasted_iota(jnp.int32, (r, 1), 0)) < nvalid
        xs_v[rows(s), :] = jnp.where(live, stage_v[rows(s), :], 0.0).astype(BF16)

    def down(s, gu, slot):
        return jnp.dot(activation(gu), w2_v[slot], preferred_element_type=F32)

    def pipelined_rows(c, slot, finish, first):
        gu_v[0] = gate_up(0, c)

        def body(s, carry):
            g = s & 1
            if first:
                load_rows(s + 1)
            nxt = gate_up(s + 1, c)
            finish(s, gu_v[g], slot)
            gu_v[1 - g] = nxt
            return carry

        lax.fori_loop(0, nsub - 1, body, 0)

    def finish_first(s, gu, slot):
        acc_v[rows(s), :] = down(s, gu, slot)

    def finish_mid(s, gu, slot):
        acc_v[rows(s), :] = acc_v[rows(s), :] + down(s, gu, slot)

    def finish_last(s, gu, slot):
        acc_v[rows(s), :] = acc_v[rows(s), :] + down(s, gu, slot) + b2_ref[0]
        out_copy(s).start()

    tail = nsub - 1

    @pl.when(i == 0)
    def _():
        start_weights(i, 0, 0)
        each(nsub, lambda s: in_copy(i, s).start())

    wait_weights(0, 0)
    start_weights(i, 1, 1)
    prep_weights(0)
    load_rows(0)

    @pl.when(i > 0)
    def _():
        each(ins_ref[jnp.maximum(i - 1, 0)], lambda s: out_copy(s).wait())

    pipelined_rows(0, 0, finish_first, first=True)

    @pl.when(has_next)
    def _():
        each(ins_ref[i + 1], lambda s: in_copy(i + 1, s).start())

    def mid_chunk(c, carry):
        slot = c & 1
        wait_weights(c, slot)
        start_weights(i, c + 1, 1 - slot)
        prev = down(tail, gu_v[tail & 1], 1 - slot)
        acc_v[rows(tail), :] = jnp.where(c == 1, prev, acc_v[rows(tail), :] + prev)
        prep_weights(slot)
        pipelined_rows(c, slot, finish_mid, first=False)
        return carry

    lax.fori_loop(1, last_c, mid_chunk, 0)

    lslot = last_c & 1
    wait_weights(last_c, lslot)

    @pl.when(has_next)
    def _():
        start_weights(i + 1, 0, 1 - lslot)

    finish_mid(tail, gu_v[tail & 1], 1 - lslot)
    prep_weights(lslot)
    pipelined_rows(last_c, lslot, finish_last, first=False)
    finish_last(tail, gu_v[tail & 1], lslot)

    @pl.when(jnp.logical_not(has_next))
    def _():
        each(nsub, lambda s: out_copy(s).wait())


def _moe(item_e, item_row0, item_nsub, item_nvalid, xs, w_gate_up, b_gate_up, w_down, b_down):
    n_items = item_e.shape[0]
    n_rows, d = xs.shape
    e = w_gate_up.shape[0]
    b1 = b_gate_up.reshape(e, 1, 2 * D_FF)
    b2 = b_down.reshape(e, 1, d)
    pad1 = lambda a: jnp.pad(a, (0, 1))

    grid_spec = pltpu.PrefetchScalarGridSpec(
        num_scalar_prefetch=4,
        grid=(n_items,),
        in_specs=[
            pl.BlockSpec(memory_space=pl.ANY),
            pl.BlockSpec(memory_space=pl.ANY),
            pl.BlockSpec((1, 1, 2 * D_FF), lambda i, ie, ir, ins, inv: (ie[i], 0, 0)),
            pl.BlockSpec(memory_space=pl.ANY),
            pl.BlockSpec((1, 1, d), lambda i, ie, ir, ins, inv: (ie[i], 0, 0)),
        ],
        out_specs=pl.BlockSpec(memory_space=pl.ANY),
        scratch_shapes=[
            pltpu.VMEM((MOE_TMAX, d), BF16),
            pltpu.VMEM((MOE_TMAX, d), F32),
            pltpu.VMEM((MOE_TMAX, d), F32),
            pltpu.VMEM((2, d, _GU_W), F32),
            pltpu.VMEM((2, MOE_FC, d), F32),
            pltpu.VMEM((d, _GU_W), BF16),
            pltpu.VMEM((d // LANES, MOE_FC, LANES), F32),
            pltpu.VMEM((2, MOE_FC, d), BF16),
            pltpu.VMEM((2, MOE_R, _GU_W), F32),
            pltpu.SemaphoreType.DMA((MOE_TMAX // MOE_R,)),
            pltpu.SemaphoreType.DMA(()),
            pltpu.SemaphoreType.DMA((2, 2)),
        ],
    )
    return pl.pallas_call(
        _moe_kernel,
        grid_spec=grid_spec,
        out_shape=jax.ShapeDtypeStruct((n_rows, d), F32),
        compiler_params=_cparams(("arbitrary",)),
        name="moe_experts",
    )(pad1(item_e), pad1(item_row0), pad1(item_nsub), pad1(item_nvalid),
      xs, w_gate_up, b1, w_down, b2)


def _combine_kernel(dest_ref, dest_next_ref, y_hbm, gw_ref, x1_ref, mod_ref, g_ref, o_ref,
                    buf, sem):
    i = pl.program_id(0)
    n = COMB_TM * TOP_K
    slot = i & 1

    def gather(d_ref, sl):
        def start(j, c):
            pltpu.make_async_copy(
                y_hbm.at[pl.ds(d_ref[0, 0, j], 1)],
                buf.at[sl, j & (TOP_K - 1), pl.ds(lax.shift_right_logical(j, _LOG2_TOP_K), 1)],
                sem.at[sl]).start()
            return c

        lax.fori_loop(0, n, start, 0, unroll=DMA_UNROLL)

    @pl.when(i == 0)
    def _():
        gather(dest_ref, 0)

    @pl.when(i + 1 < pl.num_programs(0))
    def _():
        gather(dest_next_ref, 1 - slot)

    pltpu.make_async_copy(buf.at[slot], buf.at[slot], sem.at[slot]).wait()

    gw = gw_ref[...]
    y = buf[slot, 0] * gw[:, 0:1]
    for k in range(1, TOP_K):
        y = y + buf[slot, k] * gw[:, k:k + 1]
    m = mod_ref[0]
    o_ref[...] = x1_ref[...] + m[5:6, :] * (_rms(y) * g_ref[...])


def _combine(dest, yb, gw, x1, mod6, g_post, seq):
    t, d = x1.shape
    steps = t // COMB_TM
    tiles_per_seq = seq // COMB_TM
    dest3 = dest.reshape(steps, 1, COMB_TM * TOP_K)
    return pl.pallas_call(
        _combine_kernel,
        grid=(steps,),
        in_specs=[
            pl.BlockSpec((1, 1, COMB_TM * TOP_K), lambda i: (i, 0, 0), memory_space=pltpu.SMEM),
            pl.BlockSpec((1, 1, COMB_TM * TOP_K), lambda i: (jnp.minimum(i + 1, steps - 1), 0, 0),
                         memory_space=pltpu.SMEM),
            pl.BlockSpec(memory_space=pl.ANY),
            pl.BlockSpec((COMB_TM, TOP_K), lambda i: (i, 0)),
            pl.BlockSpec((COMB_TM, d), lambda i: (i, 0)),
            pl.BlockSpec((1, N_MOD, d), lambda i: (i // tiles_per_seq, 0, 0)),
            pl.BlockSpec((1, d), lambda i: (0, 0)),
        ],
        out_specs=pl.BlockSpec((COMB_TM, d), lambda i: (i, 0)),
        out_shape=jax.ShapeDtypeStruct((t, d), F32),
        scratch_shapes=[pltpu.VMEM((2, TOP_K, COMB_TM, d), F32), pltpu.SemaphoreType.DMA((2,))],
        compiler_params=_cparams(("arbitrary",)),
        name="moe_combine",
    )(dest3, dest3, yb, gw, x1, mod6, g_post)


def _routing_tables(top_idx, n_items):
    t = top_idx.shape[0]
    sel = jnp.sum((top_idx[:, :, None] == jnp.arange(N_EXPERTS, dtype=jnp.int32)[None, None, :])
                  .astype(jnp.int32), axis=1)
    incl = jnp.cumsum(sel, axis=0)
    rank = incl - sel
    counts = incl[-1]
    pcounts = (counts + MOE_R - 1) // MOE_R * MOE_R
    pend = jnp.cumsum(pcounts)
    pstart = pend - pcounts
    dest = pstart[top_idx] + jnp.take_along_axis(rank, top_idx, axis=1)

    tiles = (pcounts + MOE_TMAX - 1) // MOE_TMAX
    tend = jnp.cumsum(tiles)
    tstart = tend - tiles
    it = jnp.arange(n_items, dtype=jnp.int32)
    ie = jnp.minimum(jnp.sum((tend[None, :] <= it[:, None]).astype(jnp.int32), axis=1),
                     N_EXPERTS - 1)
    valid = it < tend[-1]
    local = it - tstart[ie]
    row0 = pstart[ie] + local * MOE_TMAX
    nrows = jnp.clip(pcounts[ie] - local * MOE_TMAX, 0, MOE_TMAX)
    nsub = jnp.where(valid, nrows // MOE_R, 0).astype(jnp.int32)
    nvalid = jnp.where(valid, jnp.clip(counts[ie] - local * MOE_TMAX, 0, MOE_TMAX), 0)
    row0 = jnp.where(valid, row0, 0).astype(jnp.int32)
    return dest.astype(jnp.int32), ie, row0, nsub, nvalid.astype(jnp.int32)


def kernel(x, c, w_mod, b_mod, g_pre_mix, g_post_mix, w_in, b_in, attn_sinks, w_pool_mix,
           pool_scale, w_attn_branch, w_pool_branch, w_out, g_pre_ffn, g_post_ffn, w_router,
           b_router, w_gate_up, b_gate_up, w_down, b_down):
    b, s, d = x.shape
    t = b * s
    depth = w_mod.shape[0]
    n_assign = t * TOP_K
    n_items = N_EXPERTS + n_assign // MOE_TMAX
    n_rows = n_assign + N_EXPERTS * MOE_R

    x2 = x.reshape(t, d)
    c_pad = jnp.pad(c, ((0, SUBLANES - b), (0, 0)))
    for l in range(depth):
        mod = _modulation(c_pad, w_mod[l], b_mod[l][None, :])
        mod6 = mod[:b].reshape(b, N_MOD, d)

        q, kv, u, gl = _in_projection(x2, mod6, g_pre_mix[l][None, :], w_in[l].astype(BF16),
                                      b_in[l][None, :], s)
        a = _attention(q, kv, attn_sinks[l], s)
        x1, h2, top_idx, gate_w = _mix(
            x2, a, u, gl, mod6, w_pool_mix[l].astype(BF16), pool_scale[l][None, :],
            w_attn_branch[l].astype(BF16), w_pool_branch[l].astype(BF16),
            w_out[l].astype(BF16), g_post_mix[l][None, :], g_pre_ffn[l][None, :],
            _split_bf16(w_router[l]), b_router[l][None, :], s)

        dest, item_e, item_row0, item_nsub, item_nvalid = _routing_tables(top_idx, n_items)
        xs = _dispatch(dest, h2, n_rows)
        yb = _moe(item_e, item_row0, item_nsub, item_nvalid, xs, w_gate_up[l], b_gate_up[l],
                  w_down[l], b_down[l])
        x2 = _combine(dest, yb, gate_w, x1, mod6, g_post_ffn[l][None, :], s)
    return x2.reshape(b, s, d)
```

```python
import functools

import jax
import jax.numpy as jnp
from jax import lax
from jax.experimental import pallas as pl
from jax.experimental.pallas import tpu as pltpu

F32 = jnp.float32
BF16 = jnp.bfloat16

D_MODEL = 2048
HEAD_DIM = 64
N_Q_HEADS = 16
N_KV_HEADS = 4
GROUP = N_Q_HEADS // N_KV_HEADS
WINDOW = 128
Q_W = N_Q_HEADS * HEAD_DIM
KV_W = N_KV_HEADS * HEAD_DIM
POOL_WINDOWS = (2, 4, 8, 16)
POOL_W = D_MODEL // 2
POOL_GC = POOL_W // len(POOL_WINDOWS)
POOL_HALO = 16
N_EXPERTS = 32
TOP_K = 4
_LOG2_TOP_K = 2
D_FF = D_MODEL
SWIGLU_LIMIT = 7.0
SWIGLU_ALPHA = 1.702
N_MOD = 6
RMS_EPS = 1e-6
NEG_INF = -1e30

LANES = 128
SUBLANES = 8
VMEM_LIMIT_BYTES = 56 * 1024 * 1024

MOD_TN = 1024
PROJ_TM = 1024
PROJ_TN = 512
ATT_TQ = 512
MIX_TM = 256
MIX_SPLIT = 1
MOE_R = 256
MOE_TMAX = 1536
MOE_FC = 256
DISP_TM = 256
COMB_TM = 128
DMA_UNROLL = 128


def _cparams(sem):
    return pltpu.CompilerParams(dimension_semantics=sem, vmem_limit_bytes=VMEM_LIMIT_BYTES)


def _rms(x):
    return x * lax.rsqrt(jnp.mean(x * x, axis=-1, keepdims=True) + RMS_EPS)


def _mod_kernel(c_ref, w_ref, b_ref, o_ref):
    c = c_ref[...]
    s = c * jax.nn.sigmoid(c)
    o_ref[...] = jnp.dot(s.astype(BF16), w_ref[...].astype(BF16),
                         preferred_element_type=F32) + b_ref[...]


def _modulation(c_pad, w_mod, b_mod):
    rows, d = c_pad.shape
    n = w_mod.shape[1]
    return pl.pallas_call(
        _mod_kernel,
        grid=(n // MOD_TN,),
        in_specs=[
            pl.BlockSpec((rows, d), lambda j: (0, 0)),
            pl.BlockSpec((d, MOD_TN), lambda j: (0, j)),
            pl.BlockSpec((1, MOD_TN), lambda j: (0, j)),
        ],
        out_specs=pl.BlockSpec((rows, MOD_TN), lambda j: (0, j)),
        out_shape=jax.ShapeDtypeStruct((rows, n), F32),
        compiler_params=_cparams(("arbitrary",)),
        name="modulation",
    )(c_pad, w_mod, b_mod)


_Q_TILES = Q_W // PROJ_TN
_KV_TILES = (2 * KV_W) // PROJ_TN
_U_TILES = POOL_W // PROJ_TN
_G_TILES = (2 * D_MODEL) // PROJ_TN
_KV_T0 = _Q_TILES
_U_T0 = _KV_T0 + _KV_TILES
_G_T0 = _U_T0 + _U_TILES
_PROJ_TILES = _G_T0 + _G_TILES


def _proj_kernel(x_ref, mod_ref, g_ref, w_ref, b_ref, q_ref, kv_ref, u_ref, gl_ref, h_ref):
    j = pl.program_id(1)

    @pl.when(j == 0)
    def _():
        m = mod_ref[0]
        h = _rms(x_ref[...]) * g_ref[...] * (1.0 + m[1:2, :]) + m[0:1, :]
        h_ref[...] = h.astype(BF16)

    def project(o_ref):
        half = PROJ_TM // 2
        for r0 in (0, half):
            y = jnp.dot(h_ref[r0:r0 + half, :], w_ref[...], preferred_element_type=F32) + b_ref[...]
            o_ref[r0:r0 + half, :] = y.astype(o_ref.dtype)

    @pl.when(j < _KV_T0)
    def _():
        project(q_ref)

    @pl.when((j >= _KV_T0) & (j < _U_T0))
    def _():
        project(kv_ref)

    @pl.when((j >= _U_T0) & (j < _G_T0))
    def _():
        project(u_ref)

    @pl.when(j >= _G_T0)
    def _():
        project(gl_ref)


def _in_projection(x2, mod6, g_pre, w_in, b_in, seq):
    t, d = x2.shape
    tiles_per_seq = seq // PROJ_TM

    def clampj(j, lo, n):
        return jnp.clip(j - lo, 0, n - 1)

    return pl.pallas_call(
        _proj_kernel,
        grid=(t // PROJ_TM, _PROJ_TILES),
        in_specs=[
            pl.BlockSpec((PROJ_TM, d), lambda i, j: (i, 0)),
            pl.BlockSpec((1, N_MOD, d), lambda i, j: (i // tiles_per_seq, 0, 0)),
            pl.BlockSpec((1, d), lambda i, j: (0, 0)),
            pl.BlockSpec((d, PROJ_TN), lambda i, j: (0, j)),
            pl.BlockSpec((1, PROJ_TN), lambda i, j: (0, j)),
        ],
        out_specs=[
            pl.BlockSpec((PROJ_TM, PROJ_TN), lambda i, j: (i, clampj(j, 0, _Q_TILES))),
            pl.BlockSpec((PROJ_TM, PROJ_TN), lambda i, j: (i, clampj(j, _KV_T0, _KV_TILES))),
            pl.BlockSpec((PROJ_TM, PROJ_TN), lambda i, j: (i, clampj(j, _U_T0, _U_TILES))),
            pl.BlockSpec((PROJ_TM, PROJ_TN), lambda i, j: (i, clampj(j, _G_T0, _G_TILES))),
        ],
        out_shape=[
            jax.ShapeDtypeStruct((t, Q_W), BF16),
            jax.ShapeDtypeStruct((t, 2 * KV_W), BF16),
            jax.ShapeDtypeStruct((t, POOL_W), F32),
            jax.ShapeDtypeStruct((t, 2 * D_MODEL), F32),
        ],
        scratch_shapes=[pltpu.VMEM((PROJ_TM, d), BF16)],
        compiler_params=_cparams(("arbitrary", "arbitrary")),
        name="in_projection",
    )(x2, mod6, g_pre, w_in, b_in)


def _attn_kernel(sink_ref, q_ref, kvp_ref, kvc_ref, o_ref, *, blocks_per_seq):
    i = pl.program_id(0)
    nq = ATT_TQ // WINDOW
    rows = GROUP * WINDOW
    r_iota = lax.broadcasted_iota(jnp.int32, (rows, 2 * WINDOW), 0)
    k_iota = lax.broadcasted_iota(jnp.int32, (rows, 2 * WINDOW), 1)
    qi = r_iota % WINDOW
    band = (k_iota > qi) & (k_iota <= qi + WINDOW)
    g_of_row = lax.broadcasted_iota(jnp.int32, (rows, 1), 0) // WINDOW
    scale = HEAD_DIM ** -0.5

    for n in range(nq):
        if n == 0:
            first = (i % blocks_per_seq) == 0
            mask = band & ((k_iota >= WINDOW) | jnp.logical_not(first))
            kv_prev = kvp_ref[...]
        else:
            mask = band
            kv_prev = kvc_ref[(n - 1) * WINDOW:n * WINDOW, :]
        kv_cur = kvc_ref[n * WINDOW:(n + 1) * WINDOW, :]
        kv = jnp.concatenate([kv_prev, kv_cur], axis=0)
        qn = q_ref[n * WINDOW:(n + 1) * WINDOW, :]
        for h in range(N_KV_HEADS):
            k = kv[:, h * HEAD_DIM:(h + 1) * HEAD_DIM]
            v = kv[:, KV_W + h * HEAD_DIM:KV_W + (h + 1) * HEAD_DIM]
            qs = jnp.concatenate(
                [qn[:, (h * GROUP + g) * HEAD_DIM:(h * GROUP + g + 1) * HEAD_DIM]
                 for g in range(GROUP)], axis=0)
            s = lax.dot_general(qs, k, (((1,), (1,)), ((), ())),
                                preferred_element_type=F32) * scale
            s = jnp.where(mask, s, NEG_INF)
            sink = jnp.zeros((rows, 1), F32)
            for g in range(GROUP):
                sink = jnp.where(g_of_row == g, sink_ref[h * GROUP + g], sink)
            m = jnp.maximum(jnp.max(s, axis=-1, keepdims=True), sink)
            p = jnp.exp(s - m)
            denom = jnp.sum(p, axis=-1, keepdims=True) + jnp.exp(sink - m)
            o = jnp.dot(p.astype(BF16), v, preferred_element_type=F32) / denom
            for g in range(GROUP):
                c0 = (h * GROUP + g) * HEAD_DIM
                o_ref[n * WINDOW:(n + 1) * WINDOW, c0:c0 + HEAD_DIM] = (
                    o[g * WINDOW:(g + 1) * WINDOW, :].astype(BF16))


def _attention(q, kv, sinks, seq):
    t = q.shape[0]
    blocks_per_seq = seq // ATT_TQ
    wpb = ATT_TQ // WINDOW
    return pl.pallas_call(
        functools.partial(_attn_kernel, blocks_per_seq=blocks_per_seq),
        grid=(t // ATT_TQ,),
        in_specs=[
            pl.BlockSpec(memory_space=pltpu.SMEM),
            pl.BlockSpec((ATT_TQ, Q_W), lambda i: (i, 0)),
            pl.BlockSpec((WINDOW, 2 * KV_W), lambda i: (jnp.maximum(i * wpb - 1, 0), 0)),
            pl.BlockSpec((ATT_TQ, 2 * KV_W), lambda i: (i, 0)),
        ],
        out_specs=pl.BlockSpec((ATT_TQ, Q_W), lambda i: (i, 0)),
        out_shape=jax.ShapeDtypeStruct((t, Q_W), BF16),
        compiler_params=_cparams(("arbitrary",)),
        name="swa_attention",
    )(sinks, q, kv, kv)


def _mix_kernel(x_ref, a_ref, up_ref, u_ref, gl_ref, mod_ref, wpm_ref, ps_ref, wa_ref, wp_ref,
                wo_ref, gpost_ref, gffn_ref, wr_ref, br_ref,
                x1_ref, h2_ref, idx_ref, gw_ref, ubuf_ref, *, tiles_per_seq):
    i = pl.program_id(0)
    tm = MIX_TM
    seq_tile = i % tiles_per_seq
    m = mod_ref[0]

    halo = jnp.where(seq_tile == 0, 0.0, up_ref[...])
    ubuf_ref[0:POOL_HALO, :] = halo
    ubuf_ref[POOL_HALO:, :] = u_ref[...]
    hm = tm // MIX_SPLIT
    for part in range(MIX_SPLIT):
        r0 = part * hm
        rs = slice(r0, r0 + hm)
        pos = seq_tile * tm + r0 + lax.broadcasted_iota(jnp.int32, (hm, 1), 0)
        p_parts = []
        for g, w in enumerate(POOL_WINDOWS):
            c0 = g * POOL_GC
            b0 = POOL_HALO + r0
            acc = ubuf_ref[b0:b0 + hm, c0:c0 + POOL_GC]
            ug = acc
            for dlt in range(1, w):
                acc = acc + ubuf_ref[b0 - dlt:b0 - dlt + hm, c0:c0 + POOL_GC]
            cnt = jnp.minimum(pos + 1, w).astype(F32)
            pg = acc / cnt - ug
            p_parts.append(jnp.dot(pg.astype(BF16), wpm_ref[g], preferred_element_type=F32))
        p = jnp.concatenate(p_parts, axis=-1) * ps_ref[...]

        ya = jnp.dot(a_ref[rs, :], wa_ref[...], preferred_element_type=F32)
        yp = jnp.dot(p.astype(BF16), wp_ref[...], preferred_element_type=F32)
        gates_a = jax.nn.sigmoid(gl_ref[rs, 0:D_MODEL])
        gates_p = jax.nn.sigmoid(gl_ref[rs, D_MODEL:2 * D_MODEL])
        merged = gates_a * ya + gates_p * yp
        mix = jnp.dot(merged.astype(BF16), wo_ref[...], preferred_element_type=F32)
        x1 = x_ref[rs, :] + m[2:3, :] * (_rms(mix) * gpost_ref[...])
        x1_ref[rs, :] = x1

        h2 = _rms(x1) * gffn_ref[...] * (1.0 + m[4:5, :]) + m[3:4, :]
        h2_ref[rs, :] = h2

        h_hi = h2.astype(BF16)
        h_lo = (h2 - h_hi.astype(F32)).astype(BF16)
        prod = jnp.dot(jnp.concatenate([h_hi, h_lo], axis=0), wr_ref[...],
                       preferred_element_type=F32)
        logits = ((prod[0:hm, 0:N_EXPERTS] + prod[0:hm, N_EXPERTS:])
                  + (prod[hm:, 0:N_EXPERTS] + prod[hm:, N_EXPERTS:])) + br_ref[...]
        e_iota = lax.broadcasted_iota(jnp.int32, (hm, N_EXPERTS), 1)
        k_iota = lax.broadcasted_iota(jnp.int32, (hm, TOP_K), 1)
        vals = jnp.zeros((hm, TOP_K), F32)
        idxs = jnp.zeros((hm, TOP_K), jnp.int32)
        lg = logits
        for k in range(TOP_K):
            mx = jnp.max(lg, axis=-1, keepdims=True)
            ix = jnp.min(jnp.where(lg == mx, e_iota, N_EXPERTS), axis=-1, keepdims=True)
            vals = jnp.where(k_iota == k, mx, vals)
            idxs = jnp.where(k_iota == k, ix, idxs)
            lg = jnp.where(e_iota == ix, -jnp.inf, lg)
        ex = jnp.exp(vals - vals[:, 0:1])
        gw_ref[rs, :] = ex / jnp.sum(ex, axis=-1, keepdims=True)
        idx_ref[rs, :] = idxs


def _split_bf16(w):
    hi = w.astype(BF16)
    lo = (w - hi.astype(F32)).astype(BF16)
    return jnp.concatenate([hi, lo], axis=-1)


def _mix(x2, a, u, gl, mod6, wpm, ps, wa, wp, wo, gpost, gffn, wr, br, seq):
    t, d = x2.shape
    tm = MIX_TM
    tiles_per_seq = seq // tm
    hb = tm // POOL_HALO
    const2 = lambda i: (0, 0)
    return pl.pallas_call(
        functools.partial(_mix_kernel, tiles_per_seq=tiles_per_seq),
        grid=(t // tm,),
        in_specs=[
            pl.BlockSpec((tm, d), lambda i: (i, 0)),
            pl.BlockSpec((tm, Q_W), lambda i: (i, 0)),
            pl.BlockSpec((POOL_HALO, POOL_W), lambda i: (jnp.maximum(i * hb - 1, 0), 0)),
            pl.BlockSpec((tm, POOL_W), lambda i: (i, 0)),
            pl.BlockSpec((tm, 2 * d), lambda i: (i, 0)),
            pl.BlockSpec((1, N_MOD, d), lambda i: (i // tiles_per_seq, 0, 0)),
            pl.BlockSpec(wpm.shape, lambda i: (0, 0, 0), pipeline_mode=pl.Buffered(1)),
            pl.BlockSpec((1, POOL_W), const2),
            pl.BlockSpec(wa.shape, const2, pipeline_mode=pl.Buffered(1)),
            pl.BlockSpec(wp.shape, const2, pipeline_mode=pl.Buffered(1)),
            pl.BlockSpec(wo.shape, const2, pipeline_mode=pl.Buffered(1)),
            pl.BlockSpec((1, d), const2),
            pl.BlockSpec((1, d), const2),
            pl.BlockSpec(wr.shape, const2),
            pl.BlockSpec((1, N_EXPERTS), const2),
        ],
        out_specs=[
            pl.BlockSpec((tm, d), lambda i: (i, 0)),
            pl.BlockSpec((tm, d), lambda i: (i, 0)),
            pl.BlockSpec((tm, TOP_K), lambda i: (i, 0)),
            pl.BlockSpec((tm, TOP_K), lambda i: (i, 0)),
        ],
        out_shape=[
            jax.ShapeDtypeStruct((t, d), F32),
            jax.ShapeDtypeStruct((t, d), F32),
            jax.ShapeDtypeStruct((t, TOP_K), jnp.int32),
            jax.ShapeDtypeStruct((t, TOP_K), F32),
        ],
        scratch_shapes=[pltpu.VMEM((POOL_HALO + tm, POOL_W), F32)],
        compiler_params=_cparams(("arbitrary",)),
        name="mix_merge_router",
    )(x2, a, u, u, gl, mod6, wpm, ps, wa, wp, wo, gpost, gffn, wr, br)


def _dispatch_kernel(dest_ref, h_ref, xs_hbm, sem):
    n = DISP_TM * TOP_K

    tok_per_block = DMA_UNROLL // TOP_K

    def block(o, c):
        tok0 = pl.multiple_of(o * tok_per_block, tok_per_block)
        for k in range(DMA_UNROLL):
            pltpu.make_async_copy(
                h_ref.at[pl.ds(tok0 + k // TOP_K, 1)],
                xs_hbm.at[pl.ds(dest_ref[0, 0, o * DMA_UNROLL + k], 1)], sem).start()
        return c

    lax.fori_loop(0, n // DMA_UNROLL, block, 0)
    pltpu.make_async_copy(xs_hbm.at[pl.ds(0, n)], xs_hbm.at[pl.ds(0, n)], sem).wait()


def _dispatch(dest, h2, n_rows):
    t, d = h2.shape
    steps = t // DISP_TM
    dest3 = dest.reshape(steps, 1, DISP_TM * TOP_K)
    return pl.pallas_call(
        _dispatch_kernel,
        grid=(steps,),
        in_specs=[
            pl.BlockSpec((1, 1, DISP_TM * TOP_K), lambda i: (i, 0, 0), memory_space=pltpu.SMEM),
            pl.BlockSpec((DISP_TM, d), lambda i: (i, 0)),
        ],
        out_specs=pl.BlockSpec(memory_space=pl.ANY),
        out_shape=jax.ShapeDtypeStruct((n_rows, d), F32),
        scratch_shapes=[pltpu.SemaphoreType.DMA(())],
        compiler_params=_cparams(("arbitrary",)),
        name="moe_dispatch",
    )(dest3, h2)


_MOE_NC = D_FF // MOE_FC
_GU_W = 2 * MOE_FC


def _moe_kernel(ie_ref, ir_ref, ins_ref, inv_ref, xs_hbm, w1_hbm, b1_ref, w2_hbm, b2_ref, y_hbm,
                xs_v, acc_v, stage_v, w1f_v, w2f_v, w1_v, w2i_v, w2_v, gu_v,
                in_sem, out_sem, w_sem):
    i = pl.program_id(0)

    @pl.when(ins_ref[i] > 0)
    def _():
        _moe_item(i, ie_ref, ir_ref, ins_ref, inv_ref, xs_hbm, w1_hbm, b1_ref, w2_hbm, b2_ref,
                  y_hbm, xs_v, acc_v, stage_v, w1f_v, w2f_v, w1_v, w2i_v, w2_v, gu_v,
                  in_sem, out_sem, w_sem)


def _moe_item(i, ie_ref, ir_ref, ins_ref, inv_ref, xs_hbm, w1_hbm, b1_ref, w2_hbm, b2_ref, y_hbm,
              xs_v, acc_v, stage_v, w1f_v, w2f_v, w1_v, w2i_v, w2_v, gu_v,
              in_sem, out_sem, w_sem):
    row0 = ir_ref[i]
    nsub = ins_ref[i]
    nvalid = inv_ref[i]
    has_next = ins_ref[i + 1] > 0
    r = MOE_R
    last_c = _MOE_NC - 1

    def rows(s):
        return pl.ds(pl.multiple_of(s * r, r), r)

    def in_copy(item, s):
        src0 = ir_ref[item] + s * r
        return pltpu.make_async_copy(
            xs_hbm.at[pl.ds(pl.multiple_of(src0, r), r)], stage_v.at[rows(s)], in_sem.at[s])

    def out_copy(s):
        return pltpu.make_async_copy(
            acc_v.at[rows(s)], y_hbm.at[pl.ds(pl.multiple_of(row0 + s * r, r), r)], out_sem)

    def weight_copies(item, c, slot):
        e = ie_ref[item]
        return (
            pltpu.make_async_copy(
                w1_hbm.at[e, :, pl.ds(pl.multiple_of(c * _GU_W, _GU_W), _GU_W)],
                w1f_v.at[slot], w_sem.at[0, slot]),
            pltpu.make_async_copy(
                w2_hbm.at[e, pl.ds(pl.multiple_of(c * MOE_FC, MOE_FC), MOE_FC), :],
                w2f_v.at[slot], w_sem.at[1, slot]),
        )

    def start_weights(item, c, slot):
        for cp in weight_copies(item, c, slot):
            cp.start()

    def wait_weights(c, slot):
        for cp in weight_copies(i, c, slot):
            cp.wait()

    def each(n, fn):
        def body(s, carry):
            fn(s)
            return carry
        lax.fori_loop(0, n, body, 0)

    def prep_weights(slot):
        w1_v[...] = w1f_v[slot].astype(BF16)
        half = LANES // 2
        for cb in range(D_MODEL // LANES):
            cs = slice(cb * LANES, (cb + 1) * LANES)
            for blk in range(MOE_FC // LANES):
                b0 = blk * LANES
                w2i_v[cb, pl.ds(b0, half, stride=2), :] = w2f_v[slot, b0:b0 + half, cs]
                w2i_v[cb, pl.ds(b0 + 1, half, stride=2), :] = w2f_v[slot, b0 + half:b0 + LANES, cs]
            w2_v[slot, :, cs] = w2i_v[cb].astype(BF16)

    lane = lax.broadcasted_iota(jnp.int32, (r, LANES), 1)
    even = (lane % 2) == 0

    def gate_up(s, c):
        b1 = b1_ref[0, :, pl.ds(pl.multiple_of(c * _GU_W, _GU_W), _GU_W)]
        return jnp.dot(xs_v[rows(s), :], w1_v[...], preferred_element_type=F32) + b1

    def activation(gu):
        glu = jnp.minimum(gu, SWIGLU_LIMIT)
        f = glu * jax.nn.sigmoid(SWIGLU_ALPHA * glu)
        lin = jnp.clip(gu, -SWIGLU_LIMIT, SWIGLU_LIMIT) + 1.0
        parts = []
        for blk in range(MOE_FC // LANES):
            ca = 2 * blk * LANES
            cb = ca + LANES
            za = f[:, ca:ca + LANES] * pltpu.roll(lin[:, ca:ca + LANES], LANES - 1, 1)
            zb = pltpu.roll(f[:, cb:cb + LANES], 1, 1) * lin[:, cb:cb + LANES]
            parts.append(jnp.where(even, za, zb))
        return jnp.concatenate(parts, axis=-1).astype(BF16)

    def load_rows(s):
        in_copy(i, s).wait()
        live = (s * r + lax.broadcasted_iota(jnp.int32, (r, 1), 0)) < nvalid
        xs_v[rows(s), :] = jnp.where(live, stage_v[rows(s), :], 0.0).astype(BF16)

    def down(s, gu, slot):
        return jnp.dot(activation(gu), w2_v[slot], preferred_element_type=F32)

    def pipelined_rows(c, slot, finish, first):
        gu_v[0] = gate_up(0, c)

        def step(s, g):
            if first:
                load_rows(s + 1)
            nxt = gate_up(s + 1, c)
            finish(s, gu_v[g], slot)
            gu_v[1 - g] = nxt

        def pair(p, carry):
            step(2 * p, 0)
            step(2 * p + 1, 1)
            return carry

        def single(_, carry):
            step(nsub - 2, 0)
            return carry

        n_steps = nsub - 1
        lax.fori_loop(0, lax.shift_right_logical(n_steps, 1), pair, 0)
        lax.fori_loop(0, n_steps & 1, single, 0)

    def finish_first(s, gu, slot):
        acc_v[rows(s), :] = down(s, gu, slot)

    def finish_mid(s, gu, slot):
        acc_v[rows(s), :] = acc_v[rows(s), :] + down(s, gu, slot)

    def finish_last(s, gu, slot):
        acc_v[rows(s), :] = acc_v[rows(s), :] + down(s, gu, slot) + b2_ref[0]
        out_copy(s).start()

    tail = nsub - 1

    @pl.when(i == 0)
    def _():
        start_weights(i, 0, 0)
        each(nsub, lambda s: in_copy(i, s).start())

    wait_weights(0, 0)
    start_weights(i, 1, 1)
    prep_weights(0)
    load_rows(0)

    @pl.when(i > 0)
    def _():
        each(ins_ref[jnp.maximum(i - 1, 0)], lambda s: out_copy(s).wait())

    pipelined_rows(0, 0, finish_first, first=True)

    @pl.when(has_next)
    def _():
        each(ins_ref[i + 1], lambda s: in_copy(i + 1, s).start())

    def mid_chunk(c, carry):
        slot = c & 1
        wait_weights(c, slot)
        start_weights(i, c + 1, 1 - slot)
        prev = down(tail, gu_v[tail & 1], 1 - slot)
        acc_v[rows(tail), :] = jnp.where(c == 1, prev, acc_v[rows(tail), :] + prev)
        prep_weights(slot)
        pipelined_rows(c, slot, finish_mid, first=False)
        return carry

    lax.fori_loop(1, last_c, mid_chunk, 0)

    lslot = last_c & 1
    wait_weights(last_c, lslot)

    @pl.when(has_next)
    def _():
        start_weights(i + 1, 0, 1 - lslot)

    finish_mid(tail, gu_v[tail & 1], 1 - lslot)
    prep_weights(lslot)
    pipelined_rows(last_c, lslot, finish_last, first=False)
    finish_last(tail, gu_v[tail & 1], lslot)

    @pl.when(jnp.logical_not(has_next))
    def _():
        each(nsub, lambda s: out_copy(s).wait())


def _moe(item_e, item_row0, item_nsub, item_nvalid, xs, w_gate_up, b_gate_up, w_down, b_down):
    n_items = item_e.shape[0]
    n_rows, d = xs.shape
    e = w_gate_up.shape[0]
    b1 = b_gate_up.reshape(e, 1, 2 * D_FF)
    b2 = b_down.reshape(e, 1, d)
    pad1 = lambda a: jnp.pad(a, (0, 1))

    grid_spec = pltpu.PrefetchScalarGridSpec(
        num_scalar_prefetch=4,
        grid=(n_items,),
        in_specs=[
            pl.BlockSpec(memory_space=pl.ANY),
            pl.BlockSpec(memory_space=pl.ANY),
            pl.BlockSpec((1, 1, 2 * D_FF), lambda i, ie, ir, ins, inv: (ie[i], 0, 0)),
            pl.BlockSpec(memory_space=pl.ANY),
            pl.BlockSpec((1, 1, d), lambda i, ie, ir, ins, inv: (ie[i], 0, 0)),
        ],
        out_specs=pl.BlockSpec(memory_space=pl.ANY),
        scratch_shapes=[
            pltpu.VMEM((MOE_TMAX, d), BF16),
            pltpu.VMEM((MOE_TMAX, d), F32),
            pltpu.VMEM((MOE_TMAX, d), F32),
            pltpu.VMEM((2, d, _GU_W), F32),
            pltpu.VMEM((2, MOE_FC, d), F32),
            pltpu.VMEM((d, _GU_W), BF16),
            pltpu.VMEM((d // LANES, MOE_FC, LANES), F32),
            pltpu.VMEM((2, MOE_FC, d), BF16),
            pltpu.VMEM((2, MOE_R, _GU_W), F32),
            pltpu.SemaphoreType.DMA((MOE_TMAX // MOE_R,)),
            pltpu.SemaphoreType.DMA(()),
            pltpu.SemaphoreType.DMA((2, 2)),
        ],
    )
    return pl.pallas_call(
        _moe_kernel,
        grid_spec=grid_spec,
        out_shape=jax.ShapeDtypeStruct((n_rows, d), F32),
        compiler_params=_cparams(("arbitrary",)),
        name="moe_experts",
    )(pad1(item_e), pad1(item_row0), pad1(item_nsub), pad1(item_nvalid),
      xs, w_gate_up, b1, w_down, b2)


def _combine_kernel(dest_ref, dest_next_ref, y_hbm, gw_ref, x1_ref, mod_ref, g_ref, o_ref,
                    buf, sem):
    i = pl.program_id(0)
    n = COMB_TM * TOP_K
    slot = i & 1

    tok_per_block = DMA_UNROLL // TOP_K

    def gather(d_ref, sl):
        def block(o, c):
            tok0 = pl.multiple_of(o * tok_per_block, tok_per_block)
            for k in range(DMA_UNROLL):
                pltpu.make_async_copy(
                    y_hbm.at[pl.ds(d_ref[0, 0, o * DMA_UNROLL + k], 1)],
                    buf.at[sl, k % TOP_K, pl.ds(tok0 + k // TOP_K, 1)],
                    sem.at[sl]).start()
            return c

        lax.fori_loop(0, n // DMA_UNROLL, block, 0)

    @pl.when(i == 0)
    def _():
        gather(dest_ref, 0)

    @pl.when(i + 1 < pl.num_programs(0))
    def _():
        gather(dest_next_ref, 1 - slot)

    pltpu.make_async_copy(buf.at[slot], buf.at[slot], sem.at[slot]).wait()

    gw = gw_ref[...]
    y = buf[slot, 0] * gw[:, 0:1]
    for k in range(1, TOP_K):
        y = y + buf[slot, k] * gw[:, k:k + 1]
    m = mod_ref[0]
    o_ref[...] = x1_ref[...] + m[5:6, :] * (_rms(y) * g_ref[...])


def _combine(dest, yb, gw, x1, mod6, g_post, seq):
    t, d = x1.shape
    steps = t // COMB_TM
    tiles_per_seq = seq // COMB_TM
    dest3 = dest.reshape(steps, 1, COMB_TM * TOP_K)
    return pl.pallas_call(
        _combine_kernel,
        grid=(steps,),
        in_specs=[
            pl.BlockSpec((1, 1, COMB_TM * TOP_K), lambda i: (i, 0, 0), memory_space=pltpu.SMEM),
            pl.BlockSpec((1, 1, COMB_TM * TOP_K), lambda i: (jnp.minimum(i + 1, steps - 1), 0, 0),
                         memory_space=pltpu.SMEM),
            pl.BlockSpec(memory_space=pl.ANY),
            pl.BlockSpec((COMB_TM, TOP_K), lambda i: (i, 0)),
            pl.BlockSpec((COMB_TM, d), lambda i: (i, 0)),
            pl.BlockSpec((1, N_MOD, d), lambda i: (i // tiles_per_seq, 0, 0)),
            pl.BlockSpec((1, d), lambda i: (0, 0)),
        ],
        out_specs=pl.BlockSpec((COMB_TM, d), lambda i: (i, 0)),
        out_shape=jax.ShapeDtypeStruct((t, d), F32),
        scratch_shapes=[pltpu.VMEM((2, TOP_K, COMB_TM, d), F32), pltpu.SemaphoreType.DMA((2,))],
        compiler_params=_cparams(("arbitrary",)),
        name="moe_combine",
    )(dest3, dest3, yb, gw, x1, mod6, g_post)


def _routing_tables(top_idx, n_items):
    t = top_idx.shape[0]
    sel = jnp.sum((top_idx[:, :, None] == jnp.arange(N_EXPERTS, dtype=jnp.int32)[None, None, :])
                  .astype(jnp.int32), axis=1)
    incl = jnp.cumsum(sel, axis=0)
    rank = incl - sel
    counts = incl[-1]
    pcounts = (counts + MOE_R - 1) // MOE_R * MOE_R
    pend = jnp.cumsum(pcounts)
    pstart = pend - pcounts
    dest = pstart[top_idx] + jnp.take_along_axis(rank, top_idx, axis=1)

    tiles = (pcounts + MOE_TMAX - 1) // MOE_TMAX
    tend = jnp.cumsum(tiles)
    tstart = tend - tiles
    it = jnp.arange(n_items, dtype=jnp.int32)
    ie = jnp.minimum(jnp.sum((tend[None, :] <= it[:, None]).astype(jnp.int32), axis=1),
                     N_EXPERTS - 1)
    valid = it < tend[-1]
    local = it - tstart[ie]
    row0 = pstart[ie] + local * MOE_TMAX
    nrows = jnp.clip(pcounts[ie] - local * MOE_TMAX, 0, MOE_TMAX)
    nsub = jnp.where(valid, nrows // MOE_R, 0).astype(jnp.int32)
    nvalid = jnp.where(valid, jnp.clip(counts[ie] - local * MOE_TMAX, 0, MOE_TMAX), 0)
    row0 = jnp.where(valid, row0, 0).astype(jnp.int32)
    return dest.astype(jnp.int32), ie, row0, nsub, nvalid.astype(jnp.int32)


def kernel(x, c, w_mod, b_mod, g_pre_mix, g_post_mix, w_in, b_in, attn_sinks, w_pool_mix,
           pool_scale, w_attn_branch, w_pool_branch, w_out, g_pre_ffn, g_post_ffn, w_router,
           b_router, w_gate_up, b_gate_up, w_down, b_down):
    b, s, d = x.shape
    t = b * s
    depth = w_mod.shape[0]
    n_assign = t * TOP_K
    n_items = N_EXPERTS + n_assign // MOE_TMAX
    n_rows = n_assign + N_EXPERTS * MOE_R

    x2 = x.reshape(t, d)
    c_pad = jnp.pad(c, ((0, SUBLANES - b), (0, 0)))
    for l in range(depth):
        mod = _modulation(c_pad, w_mod[l], b_mod[l][None, :])
        mod6 = mod[:b].reshape(b, N_MOD, d)

        q, kv, u, gl = _in_projection(x2, mod6, g_pre_mix[l][None, :], w_in[l].astype(BF16),
                                      b_in[l][None, :], s)
        a = _attention(q, kv, attn_sinks[l], s)
        x1, h2, top_idx, gate_w = _mix(
            x2, a, u, gl, mod6, w_pool_mix[l].astype(BF16), pool_scale[l][None, :],
            w_attn_branch[l].astype(BF16), w_pool_branch[l].astype(BF16),
            w_out[l].astype(BF16), g_post_mix[l][None, :], g_pre_ffn[l][None, :],
            _split_bf16(w_router[l]), b_router[l][None, :], s)

        dest, item_e, item_row0, item_nsub, item_nvalid = _routing_tables(top_idx, n_items)
        xs = _dispatch(dest, h2, n_rows)
        yb = _moe(item_e, item_row0, item_nsub, item_nvalid, xs, w_gate_up[l], b_gate_up[l],
                  w_down[l], b_down[l])
        x2 = _combine(dest, yb, gate_w, x1, mod6, g_post_ffn[l][None, :], s)
    return x2.reshape(b, s, d)
```

```python
import functools

import jax
import jax.numpy as jnp
from jax import lax
from jax.experimental import pallas as pl
from jax.experimental.pallas import tpu as pltpu

F32 = jnp.float32
BF16 = jnp.bfloat16

D_MODEL = 2048
HEAD_DIM = 64
N_Q_HEADS = 16
N_KV_HEADS = 4
GROUP = N_Q_HEADS // N_KV_HEADS
WINDOW = 128
Q_W = N_Q_HEADS * HEAD_DIM
KV_W = N_KV_HEADS * HEAD_DIM
POOL_WINDOWS = (2, 4, 8, 16)
POOL_W = D_MODEL // 2
POOL_GC = POOL_W // len(POOL_WINDOWS)
POOL_HALO = 16
N_EXPERTS = 32
TOP_K = 4
_LOG2_TOP_K = 2
D_FF = D_MODEL
SWIGLU_LIMIT = 7.0
SWIGLU_ALPHA = 1.702
N_MOD = 6
RMS_EPS = 1e-6
NEG_INF = -1e30

LANES = 128
SUBLANES = 8
VMEM_LIMIT_BYTES = 56 * 1024 * 1024

MOD_TN = 1024
PROJ_TM = 1024
PROJ_TN = 512
PROJ_SUB = 1024
ATT_TQ = 512
MIX_TM = 256
MOE_R = 256
MOE_TMAX = 1536
MOE_FC = 256
DISP_TM = 256
COMB_TM = 128
DMA_UNROLL = 128


def _cparams(sem):
    return pltpu.CompilerParams(dimension_semantics=sem, vmem_limit_bytes=VMEM_LIMIT_BYTES)


def _rms(x):
    return x * lax.rsqrt(jnp.mean(x * x, axis=-1, keepdims=True) + RMS_EPS)


def _mod_kernel(c_ref, w_ref, b_ref, o_ref):
    c = c_ref[...]
    s = c * jax.nn.sigmoid(c)
    o_ref[...] = jnp.dot(s.astype(BF16), w_ref[...].astype(BF16),
                         preferred_element_type=F32) + b_ref[...]


def _modulation(c_pad, w_mod, b_mod):
    rows, d = c_pad.shape
    n = w_mod.shape[1]
    return pl.pallas_call(
        _mod_kernel,
        grid=(n // MOD_TN,),
        in_specs=[
            pl.BlockSpec((rows, d), lambda j: (0, 0)),
            pl.BlockSpec((d, MOD_TN), lambda j: (0, j)),
            pl.BlockSpec((1, MOD_TN), lambda j: (0, j)),
        ],
        out_specs=pl.BlockSpec((rows, MOD_TN), lambda j: (0, j)),
        out_shape=jax.ShapeDtypeStruct((rows, n), F32),
        compiler_params=_cparams(("arbitrary",)),
        name="modulation",
    )(c_pad, w_mod, b_mod)


_Q_TILES = Q_W // PROJ_TN
_KV_TILES = (2 * KV_W) // PROJ_TN
_U_TILES = POOL_W // PROJ_TN
_G_TILES = (2 * D_MODEL) // PROJ_TN
_KV_T0 = _Q_TILES
_U_T0 = _KV_T0 + _KV_TILES
_G_T0 = _U_T0 + _U_TILES
_PROJ_TILES = _G_T0 + _G_TILES


def _proj_kernel(x_ref, mod_ref, g_ref, w_ref, b_ref, q_ref, kv_ref, u_ref, gl_ref, h_ref):
    j = pl.program_id(1)

    @pl.when(j == 0)
    def _():
        m = mod_ref[0]
        h = _rms(x_ref[...]) * g_ref[...] * (1.0 + m[1:2, :]) + m[0:1, :]
        h_ref[...] = h.astype(BF16)

    def project(o_ref):
        for r0 in range(0, PROJ_TM, PROJ_SUB):
            y = jnp.dot(h_ref[r0:r0 + PROJ_SUB, :], w_ref[...],
                        preferred_element_type=F32) + b_ref[...]
            o_ref[r0:r0 + PROJ_SUB, :] = y.astype(o_ref.dtype)

    @pl.when(j < _KV_T0)
    def _():
        project(q_ref)

    @pl.when((j >= _KV_T0) & (j < _U_T0))
    def _():
        project(kv_ref)

    @pl.when((j >= _U_T0) & (j < _G_T0))
    def _():
        project(u_ref)

    @pl.when(j >= _G_T0)
    def _():
        project(gl_ref)


def _in_projection(x2, mod6, g_pre, w_in, b_in, seq):
    t, d = x2.shape
    tiles_per_seq = seq // PROJ_TM

    def clampj(j, lo, n):
        return jnp.clip(j - lo, 0, n - 1)

    return pl.pallas_call(
        _proj_kernel,
        grid=(t // PROJ_TM, _PROJ_TILES),
        in_specs=[
            pl.BlockSpec((PROJ_TM, d), lambda i, j: (i, 0)),
            pl.BlockSpec((1, N_MOD, d), lambda i, j: (i // tiles_per_seq, 0, 0)),
            pl.BlockSpec((1, d), lambda i, j: (0, 0)),
            pl.BlockSpec((d, PROJ_TN), lambda i, j: (0, j)),
            pl.BlockSpec((1, PROJ_TN), lambda i, j: (0, j)),
        ],
        out_specs=[
            pl.BlockSpec((PROJ_TM, PROJ_TN), lambda i, j: (i, clampj(j, 0, _Q_TILES))),
            pl.BlockSpec((PROJ_TM, PROJ_TN), lambda i, j: (i, clampj(j, _KV_T0, _KV_TILES))),
            pl.BlockSpec((PROJ_TM, PROJ_TN), lambda i, j: (i, clampj(j, _U_T0, _U_TILES))),
            pl.BlockSpec((PROJ_TM, PROJ_TN), lambda i, j: (i, clampj(j, _G_T0, _G_TILES))),
        ],
        out_shape=[
            jax.ShapeDtypeStruct((t, Q_W), BF16),
            jax.ShapeDtypeStruct((t, 2 * KV_W), BF16),
            jax.ShapeDtypeStruct((t, POOL_W), F32),
            jax.ShapeDtypeStruct((t, 2 * D_MODEL), F32),
        ],
        scratch_shapes=[pltpu.VMEM((PROJ_TM, d), BF16)],
        compiler_params=_cparams(("arbitrary", "arbitrary")),
        name="in_projection",
    )(x2, mod6, g_pre, w_in, b_in)


def _attn_kernel(sink_ref, q_ref, kvp_ref, kvc_ref, o_ref, *, blocks_per_seq):
    i = pl.program_id(0)
    nq = ATT_TQ // WINDOW
    rows = GROUP * WINDOW
    r_iota = lax.broadcasted_iota(jnp.int32, (rows, 2 * WINDOW), 0)
    k_iota = lax.broadcasted_iota(jnp.int32, (rows, 2 * WINDOW), 1)
    qi = r_iota % WINDOW
    band = (k_iota > qi) & (k_iota <= qi + WINDOW)
    g_of_row = lax.broadcasted_iota(jnp.int32, (rows, 1), 0) // WINDOW
    scale = HEAD_DIM ** -0.5

    for n in range(nq):
        if n == 0:
            first = (i % blocks_per_seq) == 0
            mask = band & ((k_iota >= WINDOW) | jnp.logical_not(first))
            kv_prev = kvp_ref[...]
        else:
            mask = band
            kv_prev = kvc_ref[(n - 1) * WINDOW:n * WINDOW, :]
        kv_cur = kvc_ref[n * WINDOW:(n + 1) * WINDOW, :]
        kv = jnp.concatenate([kv_prev, kv_cur], axis=0)
        qn = q_ref[n * WINDOW:(n + 1) * WINDOW, :]
        for h in range(N_KV_HEADS):
            k = kv[:, h * HEAD_DIM:(h + 1) * HEAD_DIM]
            v = kv[:, KV_W + h * HEAD_DIM:KV_W + (h + 1) * HEAD_DIM]
            qs = jnp.concatenate(
                [qn[:, (h * GROUP + g) * HEAD_DIM:(h * GROUP + g + 1) * HEAD_DIM]
                 for g in range(GROUP)], axis=0)
            s = lax.dot_general(qs, k, (((1,), (1,)), ((), ())),
                                preferred_element_type=F32) * scale
            s = jnp.where(mask, s, NEG_INF)
            sink = jnp.zeros((rows, 1), F32)
            for g in range(GROUP):
                sink = jnp.where(g_of_row == g, sink_ref[h * GROUP + g], sink)
            m = jnp.maximum(jnp.max(s, axis=-1, keepdims=True), sink)
            p = jnp.exp(s - m)
            denom = jnp.sum(p, axis=-1, keepdims=True) + jnp.exp(sink - m)
            o = jnp.dot(p.astype(BF16), v, preferred_element_type=F32) / denom
            for g in range(GROUP):
                c0 = (h * GROUP + g) * HEAD_DIM
                o_ref[n * WINDOW:(n + 1) * WINDOW, c0:c0 + HEAD_DIM] = (
                    o[g * WINDOW:(g + 1) * WINDOW, :].astype(BF16))


def _attention(q, kv, sinks, seq):
    t = q.shape[0]
    blocks_per_seq = seq // ATT_TQ
    wpb = ATT_TQ // WINDOW
    return pl.pallas_call(
        functools.partial(_attn_kernel, blocks_per_seq=blocks_per_seq),
        grid=(t // ATT_TQ,),
        in_specs=[
            pl.BlockSpec(memory_space=pltpu.SMEM),
            pl.BlockSpec((ATT_TQ, Q_W), lambda i: (i, 0)),
            pl.BlockSpec((WINDOW, 2 * KV_W), lambda i: (jnp.maximum(i * wpb - 1, 0), 0)),
            pl.BlockSpec((ATT_TQ, 2 * KV_W), lambda i: (i, 0)),
        ],
        out_specs=pl.BlockSpec((ATT_TQ, Q_W), lambda i: (i, 0)),
        out_shape=jax.ShapeDtypeStruct((t, Q_W), BF16),
        compiler_params=_cparams(("arbitrary",)),
        name="swa_attention",
    )(sinks, q, kv, kv)


def _mix_kernel(x_ref, a_ref, up_ref, u_ref, gl_ref, mod_ref, wpm_ref, ps_ref, wa_ref, wp_ref,
                wo_ref, gpost_ref, gffn_ref, wrt_ref, brt_ref,
                x1_ref, h2_ref, idx_ref, gw_ref, rank_ref, cnt_ref, ubuf_ref, base_ref,
                *, tiles_per_seq):
    i = pl.program_id(0)
    tm = MIX_TM
    ne = N_EXPERTS
    seq_tile = i % tiles_per_seq
    m = mod_ref[0]

    halo = jnp.where(seq_tile == 0, 0.0, up_ref[...])
    ubuf_ref[0:POOL_HALO, :] = halo
    ubuf_ref[POOL_HALO:, :] = u_ref[...]
    ya = jnp.dot(a_ref[...], wa_ref[...], preferred_element_type=F32)
    pos = seq_tile * tm + lax.broadcasted_iota(jnp.int32, (tm, 1), 0)
    p_parts = []
    for g, w in enumerate(POOL_WINDOWS):
        c0 = g * POOL_GC
        acc = ubuf_ref[POOL_HALO:POOL_HALO + tm, c0:c0 + POOL_GC]
        ug = acc
        for dlt in range(1, w):
            acc = acc + ubuf_ref[POOL_HALO - dlt:POOL_HALO - dlt + tm, c0:c0 + POOL_GC]
        cnt = jnp.minimum(pos + 1, w).astype(F32)
        pg = acc / cnt - ug
        p_parts.append(jnp.dot(pg.astype(BF16), wpm_ref[g], preferred_element_type=F32))
    p = jnp.concatenate(p_parts, axis=-1) * ps_ref[...]

    yp =jnp.dot(p.astype(BF16), wp_ref[...], preferred_element_type=F32)
    gates_a = jax.nn.sigmoid(gl_ref[:, 0:D_MODEL])
    gates_p = jax.nn.sigmoid(gl_ref[:, D_MODEL:2 * D_MODEL])
    merged = gates_a * ya + gates_p * yp
    mix = jnp.dot(merged.astype(BF16), wo_ref[...], preferred_element_type=F32)
    x1 = x_ref[...] + m[2:3, :] * (_rms(mix) * gpost_ref[...])
    x1_ref[...] = x1

    h2 = _rms(x1) * gffn_ref[...] * (1.0 + m[4:5, :]) + m[3:4, :]
    h2_ref[...] = h2

    h_hi = h2.astype(BF16)
    h_lo = (h2 - h_hi.astype(F32)).astype(BF16)
    prod = lax.dot_general(wrt_ref[...], jnp.concatenate([h_hi, h_lo], axis=0),
                           (((1,), (1,)), ((), ())), preferred_element_type=F32)
    lg = ((prod[0:ne, 0:tm] + prod[ne:, 0:tm]) + (prod[0:ne, tm:] + prod[ne:, tm:])) + brt_ref[...]
    e_iota = lax.broadcasted_iota(jnp.int32, (ne, tm), 0)
    vals, idxs, picks = [], [], []
    for _ in range(TOP_K):
        mx = jnp.max(lg, axis=0, keepdims=True)
        ix = jnp.min(jnp.where(lg == mx, e_iota, ne), axis=0, keepdims=True)
        pick = e_iota == ix
        vals.append(mx)
        idxs.append(ix)
        picks.append(pick)
        lg = jnp.where(pick, -jnp.inf, lg)
    ex = [jnp.exp(v - vals[0]) for v in vals]
    den = ex[0] + ex[1] + ex[2] + ex[3]
    gw_ref[...] = jnp.concatenate([e / den for e in ex], axis=0)
    idx_ref[...] = jnp.concatenate(idxs, axis=0)

    @pl.when(i == 0)
    def _():
        base_ref[...] = jnp.zeros_like(base_ref)

    sel = (picks[0] | picks[1] | picks[2] | picks[3]).astype(F32)
    earlier = (lax.broadcasted_iota(jnp.int32, (tm, tm), 0)
               < lax.broadcasted_iota(jnp.int32, (tm, tm), 1)).astype(BF16)
    rank = jnp.dot(sel.astype(BF16), earlier, preferred_element_type=F32) + base_ref[...]
    rank_ref[...] = jnp.concatenate(
        [jnp.sum(jnp.where(pk, rank, 0.0), axis=0, keepdims=True) for pk in picks],
        axis=0).astype(jnp.int32)
    total = base_ref[...] + jnp.sum(sel, axis=1, keepdims=True)
    base_ref[...] = total
    cnt_ref[...] = total


def _split_bf16_rows(w):
    hi = w.astype(BF16)
    lo = (w - hi.astype(F32)).astype(BF16)
    return jnp.concatenate([hi, lo], axis=0)


def _mix(x2, a, u, gl, mod6, wpm, ps, wa, wp, wo, gpost, gffn, wrt, brt, seq):
    t, d = x2.shape
    tm = MIX_TM
    tiles_per_seq = seq // tm
    hb = tm // POOL_HALO
    const2 = lambda i: (0, 0)
    return pl.pallas_call(
        functools.partial(_mix_kernel, tiles_per_seq=tiles_per_seq),
        grid=(t // tm,),
        in_specs=[
            pl.BlockSpec((tm, d), lambda i: (i, 0)),
            pl.BlockSpec((tm, Q_W), lambda i: (i, 0)),
            pl.BlockSpec((POOL_HALO, POOL_W), lambda i: (jnp.maximum(i * hb - 1, 0), 0)),
            pl.BlockSpec((tm, POOL_W), lambda i: (i, 0)),
            pl.BlockSpec((tm, 2 * d), lambda i: (i, 0)),
            pl.BlockSpec((1, N_MOD, d), lambda i: (i // tiles_per_seq, 0, 0)),
            pl.BlockSpec(wpm.shape, lambda i: (0, 0, 0), pipeline_mode=pl.Buffered(1)),
            pl.BlockSpec((1, POOL_W), const2),
            pl.BlockSpec(wa.shape, const2, pipeline_mode=pl.Buffered(1)),
            pl.BlockSpec(wp.shape, const2, pipeline_mode=pl.Buffered(1)),
            pl.BlockSpec(wo.shape, const2, pipeline_mode=pl.Buffered(1)),
            pl.BlockSpec((1, d), const2),
            pl.BlockSpec((1, d), const2),
            pl.BlockSpec(wrt.shape, const2),
            pl.BlockSpec((N_EXPERTS, 1), const2),
        ],
        out_specs=[
            pl.BlockSpec((tm, d), lambda i: (i, 0)),
            pl.BlockSpec((tm, d), lambda i: (i, 0)),
            pl.BlockSpec((TOP_K, tm), lambda i: (0, i)),
            pl.BlockSpec((TOP_K, tm), lambda i: (0, i)),
            pl.BlockSpec((TOP_K, tm), lambda i: (0, i)),
            pl.BlockSpec((N_EXPERTS, 1), const2),
        ],
        out_shape=[
            jax.ShapeDtypeStruct((t, d), F32),
            jax.ShapeDtypeStruct((t, d), F32),
            jax.ShapeDtypeStruct((TOP_K, t), jnp.int32),
            jax.ShapeDtypeStruct((TOP_K, t), F32),
            jax.ShapeDtypeStruct((TOP_K, t), jnp.int32),
            jax.ShapeDtypeStruct((N_EXPERTS, 1), F32),
        ],
        scratch_shapes=[pltpu.VMEM((POOL_HALO + tm, POOL_W), F32),
                        pltpu.VMEM((N_EXPERTS, 1), F32)],
        compiler_params=_cparams(("arbitrary",)),
        name="mix_merge_router",
    )(x2, a, u, u, gl, mod6, wpm, ps, wa, wp, wo, gpost, gffn, wrt, brt)


def _dispatch_kernel(dest_ref, h_ref, xs_hbm, sem):
    n = DISP_TM * TOP_K

    tok_per_block = DMA_UNROLL // TOP_K

    def block(o, c):
        tok0 = pl.multiple_of(o * tok_per_block, tok_per_block)
        for j in range(DMA_UNROLL):
            pltpu.make_async_copy(
                h_ref.at[pl.ds(tok0 + j // TOP_K, 1)],
                xs_hbm.at[pl.ds(dest_ref[0, 0, o * DMA_UNROLL + j], 1)], sem).start()
        return c

    lax.fori_loop(0, n // DMA_UNROLL, block, 0)
    pltpu.make_async_copy(xs_hbm.at[pl.ds(0, n)], xs_hbm.at[pl.ds(0, n)], sem).wait()


def _dispatch(dest3, h2, n_rows):
    t, d = h2.shape
    steps = t // DISP_TM
    return pl.pallas_call(
        _dispatch_kernel,
        grid=(steps,),
        in_specs=[
            pl.BlockSpec((1, 1, DISP_TM * TOP_K), lambda i: (i, 0, 0), memory_space=pltpu.SMEM),
            pl.BlockSpec((DISP_TM, d), lambda i: (i, 0)),
        ],
        out_specs=pl.BlockSpec(memory_space=pl.ANY),
        out_shape=jax.ShapeDtypeStruct((n_rows, d), F32),
        scratch_shapes=[pltpu.SemaphoreType.DMA(())],
        compiler_params=_cparams(("arbitrary",)),
        name="moe_dispatch",
    )(dest3, h2)


_MOE_NC = D_FF // MOE_FC
_GU_W = 2 * MOE_FC


def _moe_kernel(ie_ref, ir_ref, ins_ref, inv_ref, xs_hbm, w1_hbm, b1_ref, w2_hbm, b2_ref, y_hbm,
                xs_v, acc_v, stage_v, w1f_v, w2f_v, w1_v, w2i_v, w2_v, gu_v,
                in_sem, out_sem, w_sem):
    i = pl.program_id(0)

    @pl.when(ins_ref[i] > 0)
    def _():
        _moe_item(i, ie_ref, ir_ref, ins_ref, inv_ref, xs_hbm, w1_hbm, b1_ref, w2_hbm, b2_ref,
                  y_hbm, xs_v, acc_v, stage_v, w1f_v, w2f_v, w1_v, w2i_v, w2_v, gu_v,
                  in_sem, out_sem, w_sem)


def _moe_item(i, ie_ref, ir_ref, ins_ref, inv_ref, xs_hbm, w1_hbm, b1_ref, w2_hbm, b2_ref, y_hbm,
              xs_v, acc_v, stage_v, w1f_v, w2f_v, w1_v, w2i_v, w2_v, gu_v,
              in_sem, out_sem, w_sem):
    row0 = ir_ref[i]
    nsub = ins_ref[i]
    nvalid = inv_ref[i]
    has_next = ins_ref[i + 1] > 0
    r = MOE_R
    last_c = _MOE_NC - 1

    def rows(s):
        return pl.ds(pl.multiple_of(s * r, r), r)

    def in_copy(item, s):
        src0 = ir_ref[item] + s * r
        return pltpu.make_async_copy(
            xs_hbm.at[pl.ds(pl.multiple_of(src0, r), r)], stage_v.at[rows(s)], in_sem.at[s])

    def out_copy(s):
        return pltpu.make_async_copy(
            acc_v.at[rows(s)], y_hbm.at[pl.ds(pl.multiple_of(row0 + s * r, r), r)], out_sem)

    def weight_copies(item, c, slot):
        e = ie_ref[item]
        return (
            pltpu.make_async_copy(
                w1_hbm.at[e, :, pl.ds(pl.multiple_of(c * _GU_W, _GU_W), _GU_W)],
                w1f_v.at[slot], w_sem.at[0, slot]),
            pltpu.make_async_copy(
                w2_hbm.at[e, pl.ds(pl.multiple_of(c * MOE_FC, MOE_FC), MOE_FC), :],
                w2f_v.at[slot], w_sem.at[1, slot]),
        )

    def start_weights(item, c, slot):
        for cp in weight_copies(item, c, slot):
            cp.start()

    def wait_weights(c, slot):
        for cp in weight_copies(i, c, slot):
            cp.wait()

    def each(n, fn):
        def body(s, carry):
            fn(s)
            return carry
        lax.fori_loop(0, n, body, 0)

    def prep_weights(slot):
        w1_v[...] = w1f_v[slot].astype(BF16)
        half = LANES // 2
        for cb in range(D_MODEL // LANES):
            cs = slice(cb * LANES, (cb + 1) * LANES)
            for blk in range(MOE_FC // LANES):
                b0 = blk * LANES
                w2i_v[cb, pl.ds(b0, half, stride=2), :] = w2f_v[slot, b0:b0 + half, cs]
                w2i_v[cb, pl.ds(b0 + 1, half, stride=2), :] = w2f_v[slot, b0 + half:b0 + LANES, cs]
            w2_v[slot, :, cs] = w2i_v[cb].astype(BF16)

    lane = lax.broadcasted_iota(jnp.int32, (r, LANES), 1)
    even = (lane % 2) == 0

    def gate_up(s, c):
        b1 = b1_ref[0, :, pl.ds(pl.multiple_of(c * _GU_W, _GU_W), _GU_W)]
        return jnp.dot(xs_v[rows(s), :], w1_v[...], preferred_element_type=F32) + b1

    def activation(gu):
        glu = jnp.minimum(gu, SWIGLU_LIMIT)
        f = glu * jax.nn.sigmoid(SWIGLU_ALPHA * glu)
        lin = jnp.clip(gu, -SWIGLU_LIMIT, SWIGLU_LIMIT) + 1.0
        parts = []
        for blk in range(MOE_FC // LANES):
            ca = 2 * blk * LANES
            cb = ca + LANES
            za = f[:, ca:ca + LANES] * pltpu.roll(lin[:, ca:ca + LANES], LANES - 1, 1)
            zb = pltpu.roll(f[:, cb:cb + LANES], 1, 1) * lin[:, cb:cb + LANES]
            parts.append(jnp.where(even, za, zb))
        return jnp.concatenate(parts, axis=-1).astype(BF16)

    def load_rows(s):
        in_copy(i, s).wait()
        live = (s * r + lax.broadcasted_iota(jnp.int32, (r, 1), 0)) < nvalid
        xs_v[rows(s), :] = jnp.where(live, stage_v[rows(s), :], 0.0).astype(BF16)

    def down(s, gu, slot):
        return jnp.dot(activation(gu), w2_v[slot], preferred_element_type=F32)

    def pipelined_rows(c, slot, finish, first):
        gu_v[0] = gate_up(0, c)

        def step(s, g):
            if first:
                load_rows(s + 1)
            nxt = gate_up(s + 1, c)
            finish(s, gu_v[g], slot)
            gu_v[1 - g] = nxt

        def pair(p, carry):
            step(2 * p, 0)
            step(2 * p + 1, 1)
            return carry

        def single(_, carry):
            step(nsub - 2, 0)
            return carry

        n_steps = nsub - 1
        lax.fori_loop(0, lax.shift_right_logical(n_steps, 1), pair, 0)
        lax.fori_loop(0, n_steps & 1, single, 0)

    def finish_first(s, gu, slot):
        acc_v[rows(s), :] = down(s, gu, slot)

    def finish_mid(s, gu, slot):
        acc_v[rows(s), :] = acc_v[rows(s), :] + down(s, gu, slot)

    def finish_last(s, gu, slot):
        acc_v[rows(s), :] = acc_v[rows(s), :] + down(s, gu, slot) + b2_ref[0]
        out_copy(s).start()

    tail = nsub - 1

    @pl.when(i == 0)
    def _():
        start_weights(i, 0, 0)
        each(nsub, lambda s: in_copy(i, s).start())

    wait_weights(0, 0)
    start_weights(i, 1, 1)
    prep_weights(0)
    load_rows(0)

    @pl.when(i > 0)
    def _():
        each(ins_ref[jnp.maximum(i - 1, 0)], lambda s: out_copy(s).wait())

    pipelined_rows(0, 0, finish_first, first=True)

    @pl.when(has_next)
    def _():
        each(ins_ref[i + 1], lambda s: in_copy(i + 1, s).start())

    def mid_chunk(c, carry):
        slot = c & 1
        wait_weights(c, slot)
        start_weights(i, c + 1, 1 - slot)
        prev = down(tail, gu_v[tail & 1], 1 - slot)
        acc_v[rows(tail), :] = jnp.where(c == 1, prev, acc_v[rows(tail), :] + prev)
        prep_weights(slot)
        pipelined_rows(c, slot, finish_mid, first=False)
        return carry

    lax.fori_loop(1, last_c, mid_chunk, 0)

    lslot = last_c & 1
    wait_weights(last_c, lslot)

    @pl.when(has_next)
    def _():
        start_weights(i + 1, 0, 1 - lslot)

    finish_mid(tail, gu_v[tail & 1], 1 - lslot)
    prep_weights(lslot)
    pipelined_rows(last_c, lslot, finish_last, first=False)
    finish_last(tail, gu_v[tail & 1], lslot)

    @pl.when(jnp.logical_not(has_next))
    def _():
        each(nsub, lambda s: out_copy(s).wait())


def _moe(item_e, item_row0, item_nsub, item_nvalid, xs, w_gate_up, b_gate_up, w_down, b_down):
    n_items = item_e.shape[0]
    n_rows, d = xs.shape
    e = w_gate_up.shape[0]
    b1 = b_gate_up.reshape(e, 1, 2 * D_FF)
    b2 = b_down.reshape(e, 1, d)
    pad1 = lambda a: jnp.pad(a, (0, 1))

    grid_spec = pltpu.PrefetchScalarGridSpec(
        num_scalar_prefetch=4,
        grid=(n_items,),
        in_specs=[
            pl.BlockSpec(memory_space=pl.ANY),
            pl.BlockSpec(memory_space=pl.ANY),
            pl.BlockSpec((1, 1, 2 * D_FF), lambda i, ie, ir, ins, inv: (ie[i], 0, 0)),
            pl.BlockSpec(memory_space=pl.ANY),
            pl.BlockSpec((1, 1, d), lambda i, ie, ir, ins, inv: (ie[i], 0, 0)),
        ],
        out_specs=pl.BlockSpec(memory_space=pl.ANY),
        scratch_shapes=[
            pltpu.VMEM((MOE_TMAX, d), BF16),
            pltpu.VMEM((MOE_TMAX, d), F32),
            pltpu.VMEM((MOE_TMAX, d), F32),
            pltpu.VMEM((2, d, _GU_W), F32),
            pltpu.VMEM((2, MOE_FC, d), F32),
            pltpu.VMEM((d, _GU_W), BF16),
            pltpu.VMEM((d // LANES, MOE_FC, LANES), F32),
            pltpu.VMEM((2, MOE_FC, d), BF16),
            pltpu.VMEM((2, MOE_R, _GU_W), F32),
            pltpu.SemaphoreType.DMA((MOE_TMAX // MOE_R,)),
            pltpu.SemaphoreType.DMA(()),
            pltpu.SemaphoreType.DMA((2, 2)),
        ],
    )
    return pl.pallas_call(
        _moe_kernel,
        grid_spec=grid_spec,
        out_shape=jax.ShapeDtypeStruct((n_rows, d), F32),
        compiler_params=_cparams(("arbitrary",)),
        name="moe_experts",
    )(pad1(item_e), pad1(item_row0), pad1(item_nsub), pad1(item_nvalid),
      xs, w_gate_up, b1, w_down, b2)


def _combine_kernel(dest_ref, dest_next_ref, y_hbm, gw_ref, x1_ref, mod_ref, g_ref, o_ref,
                    buf, sem):
    i = pl.program_id(0)
    n = COMB_TM * TOP_K
    slot = i & 1

    blocks_per_k = COMB_TM // DMA_UNROLL

    def gather(d_ref, sl):
        def block(o, c):
            k = o if blocks_per_k == 1 else lax.div(o, blocks_per_k)
            tok0 = pl.multiple_of((o & (blocks_per_k - 1)) * DMA_UNROLL, DMA_UNROLL)
            for j in range(DMA_UNROLL):
                pltpu.make_async_copy(
                    y_hbm.at[pl.ds(d_ref[0, 0, o * DMA_UNROLL + j], 1)],
                    buf.at[sl, k, pl.ds(tok0 + j, 1)],
                    sem.at[sl]).start()
            return c

        lax.fori_loop(0, n // DMA_UNROLL, block, 0)

    @pl.when(i == 0)
    def _():
        gather(dest_ref, 0)

    @pl.when(i + 1 < pl.num_programs(0))
    def _():
        gather(dest_next_ref, 1 - slot)

    pltpu.make_async_copy(buf.at[slot], buf.at[slot], sem.at[slot]).wait()

    gw = gw_ref[...]
    y = buf[slot, 0] * gw[:, 0:1]
    for k in range(1, TOP_K):
        y = y + buf[slot, k] * gw[:, k:k + 1]
    m = mod_ref[0]
    o_ref[...] = x1_ref[...] + m[5:6, :] * (_rms(y) * g_ref[...])


def _combine(dest3, yb, gw, x1, mod6, g_post, seq):
    t, d = x1.shape
    steps = t // COMB_TM
    tiles_per_seq = seq // COMB_TM
    return pl.pallas_call(
        _combine_kernel,
        grid=(steps,),
        in_specs=[
            pl.BlockSpec((1, 1, COMB_TM * TOP_K), lambda i: (i, 0, 0), memory_space=pltpu.SMEM),
            pl.BlockSpec((1, 1, COMB_TM * TOP_K), lambda i: (jnp.minimum(i + 1, steps - 1), 0, 0),
                         memory_space=pltpu.SMEM),
            pl.BlockSpec(memory_space=pl.ANY),
            pl.BlockSpec((COMB_TM, TOP_K), lambda i: (i, 0)),
            pl.BlockSpec((COMB_TM, d), lambda i: (i, 0)),
            pl.BlockSpec((1, N_MOD, d), lambda i: (i // tiles_per_seq, 0, 0)),
            pl.BlockSpec((1, d), lambda i: (0, 0)),
        ],
        out_specs=pl.BlockSpec((COMB_TM, d), lambda i: (i, 0)),
        out_shape=jax.ShapeDtypeStruct((t, d), F32),
        scratch_shapes=[pltpu.VMEM((2, TOP_K, COMB_TM, d), F32), pltpu.SemaphoreType.DMA((2,))],
        compiler_params=_cparams(("arbitrary",)),
        name="moe_combine",
    )(dest3, dest3, yb, gw, x1, mod6, g_post)


def _per_step_entries(dest_kt, tm):
    k, t = dest_kt.shape
    return dest_kt.reshape(k, t // tm, tm).transpose(1, 0, 2).reshape(t // tm, 1, k * tm)


def _routing_tables(top_idx, rank, counts, n_items):
    pcounts = (counts + MOE_R - 1) // MOE_R * MOE_R
    pend = jnp.cumsum(pcounts)
    pstart = pend - pcounts
    experts = jnp.arange(N_EXPERTS, dtype=jnp.int32)
    start_of = jnp.sum(jnp.where(top_idx[:, :, None] == experts, pstart, 0), axis=-1)
    dest = start_of + rank

    tiles = (pcounts + MOE_TMAX - 1) // MOE_TMAX
    tend = jnp.cumsum(tiles)
    tstart = tend - tiles
    it = jnp.arange(n_items, dtype=jnp.int32)
    ie = jnp.minimum(jnp.sum((tend[None, :] <= it[:, None]).astype(jnp.int32), axis=1),
                     N_EXPERTS - 1)
    valid = it < tend[-1]
    local = it - tstart[ie]
    row0 = pstart[ie] + local * MOE_TMAX
    nrows = jnp.clip(pcounts[ie] - local * MOE_TMAX, 0, MOE_TMAX)
    nsub = jnp.where(valid, nrows // MOE_R, 0).astype(jnp.int32)
    nvalid = jnp.where(valid, jnp.clip(counts[ie] - local * MOE_TMAX, 0, MOE_TMAX), 0)
    row0 = jnp.where(valid, row0, 0).astype(jnp.int32)
    return dest.astype(jnp.int32), ie, row0, nsub, nvalid.astype(jnp.int32)


def kernel(x, c, w_mod, b_mod, g_pre_mix, g_post_mix, w_in, b_in, attn_sinks, w_pool_mix,
           pool_scale, w_attn_branch, w_pool_branch, w_out, g_pre_ffn, g_post_ffn, w_router,
           b_router, w_gate_up, b_gate_up, w_down, b_down):
    b, s, d = x.shape
    t = b * s
    depth = w_mod.shape[0]
    n_assign = t * TOP_K
    n_items = N_EXPERTS + n_assign // MOE_TMAX
    n_rows = n_assign + N_EXPERTS * MOE_R

    x2 = x.reshape(t, d)
    c_pad = jnp.pad(c, ((0, SUBLANES - b), (0, 0)))
    for l in range(depth):
        mod = _modulation(c_pad, w_mod[l], b_mod[l][None, :])
        mod6 = mod[:b].reshape(b, N_MOD, d)

        q, kv, u, gl = _in_projection(x2, mod6, g_pre_mix[l][None, :], w_in[l].astype(BF16),
                                      b_in[l][None, :], s)
        a = _attention(q, kv, attn_sinks[l], s)
        x1, h2, top_idx, gate_w, rank, counts = _mix(
            x2, a, u, gl, mod6, w_pool_mix[l].astype(BF16), pool_scale[l][None, :],
            w_attn_branch[l].astype(BF16), w_pool_branch[l].astype(BF16),
            w_out[l].astype(BF16), g_post_mix[l][None, :], g_pre_ffn[l][None, :],
            _split_bf16_rows(w_router[l].T), b_router[l][:, None], s)

        dest, item_e, item_row0, item_nsub, item_nvalid = _routing_tables(
            top_idx, rank, counts[:, 0].astype(jnp.int32), n_items)
        xs = _dispatch(dest.T.reshape(t // DISP_TM, 1, DISP_TM * TOP_K), h2, n_rows)
        yb = _moe(item_e, item_row0, item_nsub, item_nvalid, xs, w_gate_up[l], b_gate_up[l],
                  w_down[l], b_down[l])
        x2 = _combine(_per_step_entries(dest, COMB_TM), yb, gate_w.T, x1, mod6,
                      g_post_ffn[l][None, :], s)
    return x2.reshape(b, s, d)
```

```python
import functools

import jax
import jax.numpy as jnp
from jax import lax
from jax.experimental import pallas as pl
from jax.experimental.pallas import tpu as pltpu

F32 = jnp.float32
BF16 = jnp.bfloat16

D_MODEL = 2048
HEAD_DIM = 64
N_Q_HEADS = 16
N_KV_HEADS = 4
GROUP = N_Q_HEADS // N_KV_HEADS
WINDOW = 128
Q_W = N_Q_HEADS * HEAD_DIM
KV_W = N_KV_HEADS * HEAD_DIM
POOL_WINDOWS = (2, 4, 8, 16)
POOL_W = D_MODEL // 2
POOL_GC = POOL_W // len(POOL_WINDOWS)
POOL_HALO = 16
N_EXPERTS = 32
TOP_K = 4
_LOG2_TOP_K = 2
D_FF = D_MODEL
SWIGLU_LIMIT = 7.0
SWIGLU_ALPHA = 1.702
N_MOD = 6
RMS_EPS = 1e-6
NEG_INF = -1e30

LANES = 128
SUBLANES = 8
VMEM_LIMIT_BYTES = 56 * 1024 * 1024

MOD_TN = 1024
PROJ_TM = 1024
PROJ_TN = 512
PROJ_SUB = 1024
ATT_TQ = 512
MIX_TM = 256
MOE_R = 256
MOE_TMAX = 1536
MOE_FC = 256
DISP_TM = 256
COMB_TM = 128
DMA_UNROLL = 128
DMA_PRIORITIES = 2


def _cparams(sem):
    return pltpu.CompilerParams(dimension_semantics=sem, vmem_limit_bytes=VMEM_LIMIT_BYTES)


def _rms(x):
    return x * lax.rsqrt(jnp.mean(x * x, axis=-1, keepdims=True) + RMS_EPS)


def _mod_kernel(c_ref, w_ref, b_ref, o_ref):
    c = c_ref[...]
    s = c * jax.nn.sigmoid(c)
    o_ref[...] = jnp.dot(s.astype(BF16), w_ref[...].astype(BF16),
                         preferred_element_type=F32) + b_ref[...]


def _modulation(c_pad, w_mod, b_mod):
    rows, d = c_pad.shape
    n = w_mod.shape[1]
    return pl.pallas_call(
        _mod_kernel,
        grid=(n // MOD_TN,),
        in_specs=[
            pl.BlockSpec((rows, d), lambda j: (0, 0)),
            pl.BlockSpec((d, MOD_TN), lambda j: (0, j)),
            pl.BlockSpec((1, MOD_TN), lambda j: (0, j)),
        ],
        out_specs=pl.BlockSpec((rows, MOD_TN), lambda j: (0, j)),
        out_shape=jax.ShapeDtypeStruct((rows, n), F32),
        compiler_params=_cparams(("arbitrary",)),
        name="modulation",
    )(c_pad, w_mod, b_mod)


_Q_TILES = Q_W // PROJ_TN
_KV_TILES = (2 * KV_W) // PROJ_TN
_U_TILES = POOL_W // PROJ_TN
_G_TILES = (2 * D_MODEL) // PROJ_TN
_KV_T0 = _Q_TILES
_U_T0 = _KV_T0 + _KV_TILES
_G_T0 = _U_T0 + _U_TILES
_PROJ_TILES = _G_T0 + _G_TILES


def _proj_kernel(x_ref, mod_ref, g_ref, w_ref, b_ref, q_ref, kv_ref, u_ref, gl_ref, h_ref):
    j = pl.program_id(1)

    @pl.when(j == 0)
    def _():
        m = mod_ref[0]
        h = _rms(x_ref[...]) * g_ref[...] * (1.0 + m[1:2, :]) + m[0:1, :]
        h_ref[...] = h.astype(BF16)

    def project(o_ref):
        for r0 in range(0, PROJ_TM, PROJ_SUB):
            y = jnp.dot(h_ref[r0:r0 + PROJ_SUB, :], w_ref[...],
                        preferred_element_type=F32) + b_ref[...]
            o_ref[r0:r0 + PROJ_SUB, :] = y.astype(o_ref.dtype)

    @pl.when(j < _KV_T0)
    def _():
        project(q_ref)

    @pl.when((j >= _KV_T0) & (j < _U_T0))
    def _():
        project(kv_ref)

    @pl.when((j >= _U_T0) & (j < _G_T0))
    def _():
        project(u_ref)

    @pl.when(j >= _G_T0)
    def _():
        project(gl_ref)


def _in_projection(x2, mod6, g_pre, w_in, b_in, seq):
    t, d = x2.shape
    tiles_per_seq = seq // PROJ_TM

    def clampj(j, lo, n):
        return jnp.clip(j - lo, 0, n - 1)

    return pl.pallas_call(
        _proj_kernel,
        grid=(t // PROJ_TM, _PROJ_TILES),
        in_specs=[
            pl.BlockSpec((PROJ_TM, d), lambda i, j: (i, 0)),
            pl.BlockSpec((1, N_MOD, d), lambda i, j: (i // tiles_per_seq, 0, 0)),
            pl.BlockSpec((1, d), lambda i, j: (0, 0)),
            pl.BlockSpec((d, PROJ_TN), lambda i, j: (0, j)),
            pl.BlockSpec((1, PROJ_TN), lambda i, j: (0, j)),
        ],
        out_specs=[
            pl.BlockSpec((PROJ_TM, PROJ_TN), lambda i, j: (i, clampj(j, 0, _Q_TILES))),
            pl.BlockSpec((PROJ_TM, PROJ_TN), lambda i, j: (i, clampj(j, _KV_T0, _KV_TILES))),
            pl.BlockSpec((PROJ_TM, PROJ_TN), lambda i, j: (i, clampj(j, _U_T0, _U_TILES))),
            pl.BlockSpec((PROJ_TM, PROJ_TN), lambda i, j: (i, clampj(j, _G_T0, _G_TILES))),
        ],
        out_shape=[
            jax.ShapeDtypeStruct((t, Q_W), BF16),
            jax.ShapeDtypeStruct((t, 2 * KV_W), BF16),
            jax.ShapeDtypeStruct((t, POOL_W), F32),
            jax.ShapeDtypeStruct((t, 2 * D_MODEL), F32),
        ],
        scratch_shapes=[pltpu.VMEM((PROJ_TM, d), BF16)],
        compiler_params=_cparams(("arbitrary", "arbitrary")),
        name="in_projection",
    )(x2, mod6, g_pre, w_in, b_in)


def _attn_kernel(sink_ref, q_ref, kvp_ref, kvc_ref, o_ref, *, blocks_per_seq):
    i = pl.program_id(0)
    nq = ATT_TQ // WINDOW
    rows = GROUP * WINDOW
    r_iota = lax.broadcasted_iota(jnp.int32, (rows, 2 * WINDOW), 0)
    k_iota = lax.broadcasted_iota(jnp.int32, (rows, 2 * WINDOW), 1)
    qi = r_iota % WINDOW
    band = (k_iota > qi) & (k_iota <= qi + WINDOW)
    g_of_row = lax.broadcasted_iota(jnp.int32, (rows, 1), 0) // WINDOW
    scale = HEAD_DIM ** -0.5

    for n in range(nq):
        if n == 0:
            first = (i % blocks_per_seq) == 0
            mask = band & ((k_iota >= WINDOW) | jnp.logical_not(first))
            kv_prev = kvp_ref[...]
        else:
            mask = band
            kv_prev = kvc_ref[(n - 1) * WINDOW:n * WINDOW, :]
        kv_cur = kvc_ref[n * WINDOW:(n + 1) * WINDOW, :]
        kv = jnp.concatenate([kv_prev, kv_cur], axis=0)
        qn = q_ref[n * WINDOW:(n + 1) * WINDOW, :]
        for h in range(N_KV_HEADS):
            k = kv[:, h * HEAD_DIM:(h + 1) * HEAD_DIM]
            v = kv[:, KV_W + h * HEAD_DIM:KV_W + (h + 1) * HEAD_DIM]
            qs = jnp.concatenate(
                [qn[:, (h * GROUP + g) * HEAD_DIM:(h * GROUP + g + 1) * HEAD_DIM]
                 for g in range(GROUP)], axis=0)
            s = lax.dot_general(qs, k, (((1,), (1,)), ((), ())),
                                preferred_element_type=F32) * scale
            s = jnp.where(mask, s, NEG_INF)
            sink = jnp.zeros((rows, 1), F32)
            for g in range(GROUP):
                sink = jnp.where(g_of_row == g, sink_ref[h * GROUP + g], sink)
            m = jnp.maximum(jnp.max(s, axis=-1, keepdims=True), sink)
            p = jnp.exp(s - m)
            denom = jnp.sum(p, axis=-1, keepdims=True) + jnp.exp(sink - m)
            o = jnp.dot(p.astype(BF16), v, preferred_element_type=F32) / denom
            for g in range(GROUP):
                c0 = (h * GROUP + g) * HEAD_DIM
                o_ref[n * WINDOW:(n + 1) * WINDOW, c0:c0 + HEAD_DIM] = (
                    o[g * WINDOW:(g + 1) * WINDOW, :].astype(BF16))


def _attention(q, kv, sinks, seq):
    t = q.shape[0]
    blocks_per_seq = seq // ATT_TQ
    wpb = ATT_TQ // WINDOW
    return pl.pallas_call(
        functools.partial(_attn_kernel, blocks_per_seq=blocks_per_seq),
        grid=(t // ATT_TQ,),
        in_specs=[
            pl.BlockSpec(memory_space=pltpu.SMEM),
            pl.BlockSpec((ATT_TQ, Q_W), lambda i: (i, 0)),
            pl.BlockSpec((WINDOW, 2 * KV_W), lambda i: (jnp.maximum(i * wpb - 1, 0), 0)),
            pl.BlockSpec((ATT_TQ, 2 * KV_W), lambda i: (i, 0)),
        ],
        out_specs=pl.BlockSpec((ATT_TQ, Q_W), lambda i: (i, 0)),
        out_shape=jax.ShapeDtypeStruct((t, Q_W), BF16),
        compiler_params=_cparams(("arbitrary",)),
        name="swa_attention",
    )(sinks, q, kv, kv)


def _mix_kernel(x_ref, a_ref, up_ref, u_ref, gl_ref, mod_ref, wpm_ref, ps_ref, wa_ref, wp_ref,
                wo_ref, gpost_ref, gffn_ref, wrt_ref, brt_ref,
                x1_ref, h2_ref, idx_ref, gw_ref, rank_ref, cnt_ref, ubuf_ref, base_ref,
                *, tiles_per_seq):
    i = pl.program_id(0)
    tm = MIX_TM
    ne = N_EXPERTS
    seq_tile = i % tiles_per_seq
    m = mod_ref[0]

    halo = jnp.where(seq_tile == 0, 0.0, up_ref[...])
    ubuf_ref[0:POOL_HALO, :] = halo
    ubuf_ref[POOL_HALO:, :] = u_ref[...]
    ya = jnp.dot(a_ref[...], wa_ref[...], preferred_element_type=F32)
    pos = seq_tile * tm + lax.broadcasted_iota(jnp.int32, (tm, 1), 0)
    p_parts = []
    for g, w in enumerate(POOL_WINDOWS):
        c0 = g * POOL_GC
        acc = ubuf_ref[POOL_HALO:POOL_HALO + tm, c0:c0 + POOL_GC]
        ug = acc
        for dlt in range(1, w):
            acc = acc + ubuf_ref[POOL_HALO - dlt:POOL_HALO - dlt + tm, c0:c0 + POOL_GC]
        cnt = jnp.minimum(pos + 1, w).astype(F32)
        pg = acc / cnt - ug
        p_parts.append(jnp.dot(pg.astype(BF16), wpm_ref[g], preferred_element_type=F32))
    p = jnp.concatenate(p_parts, axis=-1) * ps_ref[...]

    yp =jnp.dot(p.astype(BF16), wp_ref[...], preferred_element_type=F32)
    gates_a = jax.nn.sigmoid(gl_ref[:, 0:D_MODEL])
    gates_p = jax.nn.sigmoid(gl_ref[:, D_MODEL:2 * D_MODEL])
    merged = gates_a * ya + gates_p * yp
    mix = jnp.dot(merged.astype(BF16), wo_ref[...], preferred_element_type=F32)
    x1 = x_ref[...] + m[2:3, :] * (_rms(mix) * gpost_ref[...])
    x1_ref[...] = x1

    h2 = _rms(x1) * gffn_ref[...] * (1.0 + m[4:5, :]) + m[3:4, :]
    h2_ref[...] = h2

    h_hi = h2.astype(BF16)
    h_lo = (h2 - h_hi.astype(F32)).astype(BF16)
    prod = lax.dot_general(wrt_ref[...], jnp.concatenate([h_hi, h_lo], axis=0),
                           (((1,), (1,)), ((), ())), preferred_element_type=F32)
    lg = ((prod[0:ne, 0:tm] + prod[ne:, 0:tm]) + (prod[0:ne, tm:] + prod[ne:, tm:])) + brt_ref[...]
    e_iota = lax.broadcasted_iota(jnp.int32, (ne, tm), 0)
    vals, idxs, picks = [], [], []
    for _ in range(TOP_K):
        mx = jnp.max(lg, axis=0, keepdims=True)
        ix = jnp.min(jnp.where(lg == mx, e_iota, ne), axis=0, keepdims=True)
        pick = e_iota == ix
        vals.append(mx)
        idxs.append(ix)
        picks.append(pick)
        lg = jnp.where(pick, -jnp.inf, lg)
    ex = [jnp.exp(v - vals[0]) for v in vals]
    den = ex[0] + ex[1] + ex[2] + ex[3]
    gw_ref[...] = jnp.concatenate([e / den for e in ex], axis=0)
    idx_ref[...] = jnp.concatenate(idxs, axis=0)

    @pl.when(i == 0)
    def _():
        base_ref[...] = jnp.zeros_like(base_ref)

    sel = (picks[0] | picks[1] | picks[2] | picks[3]).astype(F32)
    earlier = (lax.broadcasted_iota(jnp.int32, (tm, tm), 0)
               < lax.broadcasted_iota(jnp.int32, (tm, tm), 1)).astype(BF16)
    rank = jnp.dot(sel.astype(BF16), earlier, preferred_element_type=F32) + base_ref[...]
    rank_ref[...] = jnp.concatenate(
        [jnp.sum(jnp.where(pk, rank, 0.0), axis=0, keepdims=True) for pk in picks],
        axis=0).astype(jnp.int32)
    total = base_ref[...] + jnp.sum(sel, axis=1, keepdims=True)
    base_ref[...] = total
    cnt_ref[...] = total


def _split_bf16_rows(w):
    hi = w.astype(BF16)
    lo = (w - hi.astype(F32)).astype(BF16)
    return jnp.concatenate([hi, lo], axis=0)


def _mix(x2, a, u, gl, mod6, wpm, ps, wa, wp, wo, gpost, gffn, wrt, brt, seq):
    t, d = x2.shape
    tm = MIX_TM
    tiles_per_seq = seq // tm
    hb = tm // POOL_HALO
    const2 = lambda i: (0, 0)
    return pl.pallas_call(
        functools.partial(_mix_kernel, tiles_per_seq=tiles_per_seq),
        grid=(t // tm,),
        in_specs=[
            pl.BlockSpec((tm, d), lambda i: (i, 0)),
            pl.BlockSpec((tm, Q_W), lambda i: (i, 0)),
            pl.BlockSpec((POOL_HALO, POOL_W), lambda i: (jnp.maximum(i * hb - 1, 0), 0)),
            pl.BlockSpec((tm, POOL_W), lambda i: (i, 0)),
            pl.BlockSpec((tm, 2 * d), lambda i: (i, 0)),
            pl.BlockSpec((1, N_MOD, d), lambda i: (i // tiles_per_seq, 0, 0)),
            pl.BlockSpec(wpm.shape, lambda i: (0, 0, 0), pipeline_mode=pl.Buffered(1)),
            pl.BlockSpec((1, POOL_W), const2),
            pl.BlockSpec(wa.shape, const2, pipeline_mode=pl.Buffered(1)),
            pl.BlockSpec(wp.shape, const2, pipeline_mode=pl.Buffered(1)),
            pl.BlockSpec(wo.shape, const2, pipeline_mode=pl.Buffered(1)),
            pl.BlockSpec((1, d), const2),
            pl.BlockSpec((1, d), const2),
            pl.BlockSpec(wrt.shape, const2),
            pl.BlockSpec((N_EXPERTS, 1), const2),
        ],
        out_specs=[
            pl.BlockSpec((tm, d), lambda i: (i, 0)),
            pl.BlockSpec((tm, d), lambda i: (i, 0)),
            pl.BlockSpec((TOP_K, tm), lambda i: (0, i)),
            pl.BlockSpec((TOP_K, tm), lambda i: (0, i)),
            pl.BlockSpec((TOP_K, tm), lambda i: (0, i)),
            pl.BlockSpec((N_EXPERTS, 1), const2),
        ],
        out_shape=[
            jax.ShapeDtypeStruct((t, d), F32),
            jax.ShapeDtypeStruct((t, d), F32),
            jax.ShapeDtypeStruct((TOP_K, t), jnp.int32),
            jax.ShapeDtypeStruct((TOP_K, t), F32),
            jax.ShapeDtypeStruct((TOP_K, t), jnp.int32),
            jax.ShapeDtypeStruct((N_EXPERTS, 1), F32),
        ],
        scratch_shapes=[pltpu.VMEM((POOL_HALO + tm, POOL_W), F32),
                        pltpu.VMEM((N_EXPERTS, 1), F32)],
        compiler_params=_cparams(("arbitrary",)),
        name="mix_merge_router",
    )(x2, a, u, u, gl, mod6, wpm, ps, wa, wp, wo, gpost, gffn, wrt, brt)


def _dispatch_kernel(dest_ref, h_ref, xs_hbm, sem):
    n = DISP_TM * TOP_K

    tok_per_block = DMA_UNROLL // TOP_K

    def block(o, c):
        tok0 = pl.multiple_of(o * tok_per_block, tok_per_block)
        for j in range(DMA_UNROLL):
            pltpu.make_async_copy(
                h_ref.at[pl.ds(tok0 + j // TOP_K, 1)],
                xs_hbm.at[pl.ds(dest_ref[0, 0, o * DMA_UNROLL + j], 1)],
                sem).start(priority=j % DMA_PRIORITIES)
        return c

    lax.fori_loop(0, n // DMA_UNROLL, block, 0)
    pltpu.make_async_copy(xs_hbm.at[pl.ds(0, n)], xs_hbm.at[pl.ds(0, n)], sem).wait()


def _dispatch(dest3, h2, n_rows):
    t, d = h2.shape
    steps = t // DISP_TM
    return pl.pallas_call(
        _dispatch_kernel,
        grid=(steps,),
        in_specs=[
            pl.BlockSpec((1, 1, DISP_TM * TOP_K), lambda i: (i, 0, 0), memory_space=pltpu.SMEM),
            pl.BlockSpec((DISP_TM, d), lambda i: (i, 0)),
        ],
        out_specs=pl.BlockSpec(memory_space=pl.ANY),
        out_shape=jax.ShapeDtypeStruct((n_rows, d), F32),
        scratch_shapes=[pltpu.SemaphoreType.DMA(())],
        compiler_params=_cparams(("arbitrary",)),
        name="moe_dispatch",
    )(dest3, h2)


_MOE_NC = D_FF // MOE_FC
_GU_W = 2 * MOE_FC


def _moe_kernel(ie_ref, ir_ref, ins_ref, inv_ref, xs_hbm, w1_hbm, b1_ref, w2_hbm, b2_ref, y_hbm,
                xs_v, acc_v, stage_v, w1f_v, w2f_v, w1_v, w2i_v, w2_v, gu_v,
                in_sem, out_sem, w_sem):
    i = pl.program_id(0)

    @pl.when(ins_ref[i] > 0)
    def _():
        _moe_item(i, ie_ref, ir_ref, ins_ref, inv_ref, xs_hbm, w1_hbm, b1_ref, w2_hbm, b2_ref,
                  y_hbm, xs_v, acc_v, stage_v, w1f_v, w2f_v, w1_v, w2i_v, w2_v, gu_v,
                  in_sem, out_sem, w_sem)


def _moe_item(i, ie_ref, ir_ref, ins_ref, inv_ref, xs_hbm, w1_hbm, b1_ref, w2_hbm, b2_ref, y_hbm,
              xs_v, acc_v, stage_v, w1f_v, w2f_v, w1_v, w2i_v, w2_v, gu_v,
              in_sem, out_sem, w_sem):
    row0 = ir_ref[i]
    nsub = ins_ref[i]
    nvalid = inv_ref[i]
    has_next = ins_ref[i + 1] > 0
    r = MOE_R
    last_c = _MOE_NC - 1

    def rows(s):
        return pl.ds(pl.multiple_of(s * r, r), r)

    def in_copy(item, s):
        src0 = ir_ref[item] + s * r
        return pltpu.make_async_copy(
            xs_hbm.at[pl.ds(pl.multiple_of(src0, r), r)], stage_v.at[rows(s)], in_sem.at[s])

    def out_copy(s):
        return pltpu.make_async_copy(
            acc_v.at[rows(s)], y_hbm.at[pl.ds(pl.multiple_of(row0 + s * r, r), r)], out_sem)

    def weight_copies(item, c, slot):
        e = ie_ref[item]
        return (
            pltpu.make_async_copy(
                w1_hbm.at[e, :, pl.ds(pl.multiple_of(c * _GU_W, _GU_W), _GU_W)],
                w1f_v.at[slot], w_sem.at[0, slot]),
            pltpu.make_async_copy(
                w2_hbm.at[e, pl.ds(pl.multiple_of(c * MOE_FC, MOE_FC), MOE_FC), :],
                w2f_v.at[slot], w_sem.at[1, slot]),
        )

    def start_weights(item, c, slot):
        for cp in weight_copies(item, c, slot):
            cp.start()

    def wait_weights(c, slot):
        for cp in weight_copies(i, c, slot):
            cp.wait()

    def each(n, fn):
        def body(s, carry):
            fn(s)
            return carry
        lax.fori_loop(0, n, body, 0)

    def prep_w1(slot):
        w1_v[...] = w1f_v[slot].astype(BF16)

    def prep_w2(slot):
        half = LANES // 2
        for cb in range(D_MODEL // LANES):
            cs = slice(cb * LANES, (cb + 1) * LANES)
            for blk in range(MOE_FC // LANES):
                b0 = blk * LANES
                w2i_v[cb, pl.ds(b0, half, stride=2), :] = w2f_v[slot, b0:b0 + half, cs]
                w2i_v[cb, pl.ds(b0 + 1, half, stride=2), :] = w2f_v[slot, b0 + half:b0 + LANES, cs]
            w2_v[slot, :, cs] = w2i_v[cb].astype(BF16)

    lane = lax.broadcasted_iota(jnp.int32, (r, LANES), 1)
    even = (lane % 2) == 0

    def gate_up(s, c):
        b1 = b1_ref[0, :, pl.ds(pl.multiple_of(c * _GU_W, _GU_W), _GU_W)]
        return jnp.dot(xs_v[rows(s), :], w1_v[...], preferred_element_type=F32) + b1

    def activation(gu):
        glu = jnp.minimum(gu, SWIGLU_LIMIT)
        f = glu * jax.nn.sigmoid(SWIGLU_ALPHA * glu)
        lin = jnp.clip(gu, -SWIGLU_LIMIT, SWIGLU_LIMIT) + 1.0
        parts = []
        for blk in range(MOE_FC // LANES):
            ca = 2 * blk * LANES
            cb = ca + LANES
            za = f[:, ca:ca + LANES] * pltpu.roll(lin[:, ca:ca + LANES], LANES - 1, 1)
            zb = pltpu.roll(f[:, cb:cb + LANES], 1, 1) * lin[:, cb:cb + LANES]
            parts.append(jnp.where(even, za, zb))
        return jnp.concatenate(parts, axis=-1).astype(BF16)

    def load_rows(s):
        in_copy(i, s).wait()
        live = (s * r + lax.broadcasted_iota(jnp.int32, (r, 1), 0)) < nvalid
        xs_v[rows(s), :] = jnp.where(live, stage_v[rows(s), :], 0.0).astype(BF16)

    def down(s, gu, slot):
        return jnp.dot(activation(gu), w2_v[slot], preferred_element_type=F32)

    def pipelined_rows(c, slot, finish, first, tslot, after_first):
        g0 = gate_up(0, c)
        gu_v[0] = g0
        gu_v[tslot] = g0
        after_first()

        def step(s, g):
            if first:
                load_rows(s + 1)
            nxt = gate_up(s + 1, c)
            finish(s, gu_v[g], slot)
            gu_v[1 - g] = nxt
            gu_v[tslot] = nxt

        def pair(p, carry):
            step(2 * p, 0)
            step(2 * p + 1, 1)
            return carry

        def single(_, carry):
            step(nsub - 2, 0)
            return carry

        n_steps = nsub - 1
        lax.fori_loop(0, lax.shift_right_logical(n_steps, 1), pair, 0)
        lax.fori_loop(0, n_steps & 1, single, 0)

    def finish_first(s, gu, slot):
        acc_v[rows(s), :] = down(s, gu, slot)

    def finish_mid(s, gu, slot):
        acc_v[rows(s), :] = acc_v[rows(s), :] + down(s, gu, slot)

    def finish_last(s, gu, slot):
        acc_v[rows(s), :] = acc_v[rows(s), :] + down(s, gu, slot) + b2_ref[0]
        out_copy(s).start()

    tail = nsub - 1

    @pl.when(i == 0)
    def _():
        start_weights(i, 0, 0)
        each(nsub, lambda s: in_copy(i, s).start())

    wait_weights(0, 0)
    start_weights(i, 1, 1)

    @pl.when(i > 0)
    def _():
        each(ins_ref[jnp.maximum(i - 1, 0)], lambda s: out_copy(s).wait())

    prep_w1(0)
    load_rows(0)
    pipelined_rows(0, 0, finish_first, True, 2, lambda: prep_w2(0))

    @pl.when(has_next)
    def _():
        each(ins_ref[i + 1], lambda s: in_copy(i + 1, s).start())

    def mid_chunk(c, slot):
        wait_weights(c, slot)
        start_weights(i, c + 1, 1 - slot)
        prep_w1(slot)

        def after_first():
            prev = down(tail, gu_v[3 - slot], 1 - slot)
            acc_v[rows(tail), :] = jnp.where(c == 1, prev, acc_v[rows(tail), :] + prev)
            prep_w2(slot)

        pipelined_rows(c, slot, finish_mid, False, 2 + slot, after_first)

    def chunk_pair(p, carry):
        mid_chunk(2 * p + 1, 1)
        mid_chunk(2 * p + 2, 0)
        return carry

    lax.fori_loop(0, (last_c - 1) // 2, chunk_pair, 0)

    lslot = last_c & 1
    wait_weights(last_c, lslot)

    @pl.when(has_next)
    def _():
        start_weights(i + 1, 0, 1 - lslot)

    prep_w1(lslot)

    def last_after_first():
        finish_mid(tail, gu_v[3 - lslot], 1 - lslot)
        prep_w2(lslot)

    pipelined_rows(last_c, lslot, finish_last, False, 2 + lslot, last_after_first)
    finish_last(tail, gu_v[2 + lslot], lslot)

    @pl.when(jnp.logical_not(has_next))
    def _():
        each(nsub, lambda s: out_copy(s).wait())


def _moe(item_e, item_row0, item_nsub, item_nvalid, xs, w_gate_up, b_gate_up, w_down, b_down):
    n_items = item_e.shape[0]
    n_rows, d = xs.shape
    e = w_gate_up.shape[0]
    b1 = b_gate_up.reshape(e, 1, 2 * D_FF)
    b2 = b_down.reshape(e, 1, d)
    pad1 = lambda a: jnp.pad(a, (0, 1))

    grid_spec = pltpu.PrefetchScalarGridSpec(
        num_scalar_prefetch=4,
        grid=(n_items,),
        in_specs=[
            pl.BlockSpec(memory_space=pl.ANY),
            pl.BlockSpec(memory_space=pl.ANY),
            pl.BlockSpec((1, 1, 2 * D_FF), lambda i, ie, ir, ins, inv: (ie[i], 0, 0)),
            pl.BlockSpec(memory_space=pl.ANY),
            pl.BlockSpec((1, 1, d), lambda i, ie, ir, ins, inv: (ie[i], 0, 0)),
        ],
        out_specs=pl.BlockSpec(memory_space=pl.ANY),
        scratch_shapes=[
            pltpu.VMEM((MOE_TMAX, d), BF16),
            pltpu.VMEM((MOE_TMAX, d), F32),
            pltpu.VMEM((MOE_TMAX, d), F32),
            pltpu.VMEM((2, d, _GU_W), F32),
            pltpu.VMEM((2, MOE_FC, d), F32),
            pltpu.VMEM((d, _GU_W), BF16),
            pltpu.VMEM((d // LANES, MOE_FC, LANES), F32),
            pltpu.VMEM((2, MOE_FC, d), BF16),
            pltpu.VMEM((4, MOE_R, _GU_W), F32),
            pltpu.SemaphoreType.DMA((MOE_TMAX // MOE_R,)),
            pltpu.SemaphoreType.DMA(()),
            pltpu.SemaphoreType.DMA((2, 2)),
        ],
    )
    return pl.pallas_call(
        _moe_kernel,
        grid_spec=grid_spec,
        out_shape=jax.ShapeDtypeStruct((n_rows, d), F32),
        compiler_params=_cparams(("arbitrary",)),
        name="moe_experts",
    )(pad1(item_e), pad1(item_row0), pad1(item_nsub), pad1(item_nvalid),
      xs, w_gate_up, b1, w_down, b2)


def _combine_kernel(dest_ref, dest_next_ref, y_hbm, gw_ref, x1_ref, mod_ref, g_ref, o_ref,
                    buf, sem):
    i = pl.program_id(0)
    n = COMB_TM * TOP_K
    slot = i & 1

    blocks_per_k = COMB_TM // DMA_UNROLL

    def gather(d_ref, sl):
        def block(o, c):
            k = o if blocks_per_k == 1 else lax.div(o, blocks_per_k)
            tok0 = pl.multiple_of((o & (blocks_per_k - 1)) * DMA_UNROLL, DMA_UNROLL)
            for j in range(DMA_UNROLL):
                pltpu.make_async_copy(
                    y_hbm.at[pl.ds(d_ref[0, 0, o * DMA_UNROLL + j], 1)],
                    buf.at[sl, k, pl.ds(tok0 + j, 1)],
                    sem.at[sl]).start(priority=j % DMA_PRIORITIES)
            return c

        lax.fori_loop(0, n // DMA_UNROLL, block, 0)

    @pl.when(i == 0)
    def _():
        gather(dest_ref, 0)

    @pl.when(i + 1 < pl.num_programs(0))
    def _():
        gather(dest_next_ref, 1 - slot)

    pltpu.make_async_copy(buf.at[slot], buf.at[slot], sem.at[slot]).wait()

    gw = gw_ref[...]
    y = buf[slot, 0] * gw[:, 0:1]
    for k in range(1, TOP_K):
        y = y + buf[slot, k] * gw[:, k:k + 1]
    m = mod_ref[0]
    o_ref[...] = x1_ref[...] + m[5:6, :] * (_rms(y) * g_ref[...])


def _combine(dest3, yb, gw, x1, mod6, g_post, seq):
    t, d = x1.shape
    steps = t // COMB_TM
    tiles_per_seq = seq // COMB_TM
    return pl.pallas_call(
        _combine_kernel,
        grid=(steps,),
        in_specs=[
            pl.BlockSpec((1, 1, COMB_TM * TOP_K), lambda i: (i, 0, 0), memory_space=pltpu.SMEM),
            pl.BlockSpec((1, 1, COMB_TM * TOP_K), lambda i: (jnp.minimum(i + 1, steps - 1), 0, 0),
                         memory_space=pltpu.SMEM),
            pl.BlockSpec(memory_space=pl.ANY),
            pl.BlockSpec((COMB_TM, TOP_K), lambda i: (i, 0)),
            pl.BlockSpec((COMB_TM, d), lambda i: (i, 0)),
            pl.BlockSpec((1, N_MOD, d), lambda i: (i // tiles_per_seq, 0, 0)),
            pl.BlockSpec((1, d), lambda i: (0, 0)),
        ],
        out_specs=pl.BlockSpec((COMB_TM, d), lambda i: (i, 0)),
        out_shape=jax.ShapeDtypeStruct((t, d), F32),
        scratch_shapes=[pltpu.VMEM((2, TOP_K, COMB_TM, d), F32), pltpu.SemaphoreType.DMA((2,))],
        compiler_params=_cparams(("arbitrary",)),
        name="moe_combine",
    )(dest3, dest3, yb, gw, x1, mod6, g_post)


def _per_step_entries(dest_kt, tm):
    k, t = dest_kt.shape
    return dest_kt.reshape(k, t // tm, tm).transpose(1, 0, 2).reshape(t // tm, 1, k * tm)


def _routing_tables(top_idx, rank, counts, n_items):
    pcounts = (counts + MOE_R - 1) // MOE_R * MOE_R
    pend = jnp.cumsum(pcounts)
    pstart = pend - pcounts
    experts = jnp.arange(N_EXPERTS, dtype=jnp.int32)
    start_of = jnp.sum(jnp.where(top_idx[:, :, None] == experts, pstart, 0), axis=-1)
    dest = start_of + rank

    tiles = (pcounts + MOE_TMAX - 1) // MOE_TMAX
    tend = jnp.cumsum(tiles)
    tstart = tend - tiles
    it = jnp.arange(n_items, dtype=jnp.int32)
    ie = jnp.minimum(jnp.sum((tend[None, :] <= it[:, None]).astype(jnp.int32), axis=1),
                     N_EXPERTS - 1)
    valid = it < tend[-1]
    local = it - tstart[ie]
    row0 = pstart[ie] + local * MOE_TMAX
    nrows = jnp.clip(pcounts[ie] - local * MOE_TMAX, 0, MOE_TMAX)
    nsub = jnp.where(valid, nrows // MOE_R, 0).astype(jnp.int32)
    nvalid = jnp.where(valid, jnp.clip(counts[ie] - local * MOE_TMAX, 0, MOE_TMAX), 0)
    row0 = jnp.where(valid, row0, 0).astype(jnp.int32)
    return dest.astype(jnp.int32), ie, row0, nsub, nvalid.astype(jnp.int32)


def kernel(x, c, w_mod, b_mod, g_pre_mix, g_post_mix, w_in, b_in, attn_sinks, w_pool_mix,
           pool_scale, w_attn_branch, w_pool_branch, w_out, g_pre_ffn, g_post_ffn, w_router,
           b_router, w_gate_up, b_gate_up, w_down, b_down):
    b, s, d = x.shape
    t = b * s
    depth = w_mod.shape[0]
    n_assign = t * TOP_K
    n_items = N_EXPERTS + n_assign // MOE_TMAX
    n_rows = n_assign + N_EXPERTS * MOE_R

    x2 = x.reshape(t, d)
    c_pad = jnp.pad(c, ((0, SUBLANES - b), (0, 0)))
    for l in range(depth):
        mod = _modulation(c_pad, w_mod[l], b_mod[l][None, :])
        mod6 = mod[:b].reshape(b, N_MOD, d)

        q, kv, u, gl = _in_projection(x2, mod6, g_pre_mix[l][None, :], w_in[l].astype(BF16),
                                      b_in[l][None, :], s)
        a = _attention(q, kv, attn_sinks[l], s)
        x1, h2, top_idx, gate_w, rank, counts = _mix(
            x2, a, u, gl, mod6, w_pool_mix[l].astype(BF16), pool_scale[l][None, :],
            w_attn_branch[l].astype(BF16), w_pool_branch[l].astype(BF16),
            w_out[l].astype(BF16), g_post_mix[l][None, :], g_pre_ffn[l][None, :],
            _split_bf16_rows(w_router[l].T), b_router[l][:, None], s)

        dest, item_e, item_row0, item_nsub, item_nvalid = _routing_tables(
            top_idx, rank, counts[:, 0].astype(jnp.int32), n_items)
        xs = _dispatch(dest.T.reshape(t // DISP_TM, 1, DISP_TM * TOP_K), h2, n_rows)
        yb = _moe(item_e, item_row0, item_nsub, item_nvalid, xs, w_gate_up[l], b_gate_up[l],
                  w_down[l], b_down[l])
        x2 = _combine(_per_step_entries(dest, COMB_TM), yb, gate_w.T, x1, mod6,
                      g_post_ffn[l][None, :], s)
    return x2.reshape(b, s, d)
```

```python
import functools

import jax
import jax.numpy as jnp
from jax import lax
from jax.experimental import pallas as pl
from jax.experimental.pallas import tpu as pltpu

F32 = jnp.float32
BF16 = jnp.bfloat16

D_MODEL = 2048
HEAD_DIM = 64
N_Q_HEADS = 16
N_KV_HEADS = 4
GROUP = N_Q_HEADS // N_KV_HEADS
WINDOW = 128
ATTN_SCALE = HEAD_DIM ** -0.5
assert ATTN_SCALE == 0.125
Q_W = N_Q_HEADS * HEAD_DIM
KV_W = N_KV_HEADS * HEAD_DIM
POOL_WINDOWS = (2, 4, 8, 16)
POOL_W = D_MODEL // 2
POOL_GC = POOL_W // len(POOL_WINDOWS)
POOL_HALO = 16
N_EXPERTS = 32
TOP_K = 4
_LOG2_TOP_K = 2
D_FF = D_MODEL
SWIGLU_LIMIT = 7.0
SWIGLU_ALPHA = 1.702
N_MOD = 6
RMS_EPS = 1e-6
NEG_INF = -1e30

LANES = 128
SUBLANES = 8
VMEM_LIMIT_BYTES = 56 * 1024 * 1024

MOD_TN = 1024
PROJ_TM = 1024
PROJ_TN = 512
PROJ_SUB = 1024
ATT_TQ = 512
MIX_TM = 256
MOE_R = 256
MOE_TMAX = 1536
MOE_FC = 256
DISP_TM = 256
COMB_TM = 128
DMA_UNROLL = 128
DMA_PRIORITIES = 2


def _cparams(sem):
    return pltpu.CompilerParams(dimension_semantics=sem, vmem_limit_bytes=VMEM_LIMIT_BYTES)


def _rms(x):
    return x * lax.rsqrt(jnp.mean(x * x, axis=-1, keepdims=True) + RMS_EPS)


def _mod_kernel(c_ref, w_ref, b_ref, o_ref):
    c = c_ref[...]
    s = c * jax.nn.sigmoid(c)
    o_ref[...] = jnp.dot(s.astype(BF16), w_ref[...].astype(BF16),
                         preferred_element_type=F32) + b_ref[...]


def _modulation(c_pad, w_mod, b_mod):
    rows, d = c_pad.shape
    n = w_mod.shape[1]
    return pl.pallas_call(
        _mod_kernel,
        grid=(n // MOD_TN,),
        in_specs=[
            pl.BlockSpec((rows, d), lambda j: (0, 0)),
            pl.BlockSpec((d, MOD_TN), lambda j: (0, j)),
            pl.BlockSpec((1, MOD_TN), lambda j: (0, j)),
        ],
        out_specs=pl.BlockSpec((rows, MOD_TN), lambda j: (0, j)),
        out_shape=jax.ShapeDtypeStruct((rows, n), F32),
        compiler_params=_cparams(("arbitrary",)),
        name="modulation",
    )(c_pad, w_mod, b_mod)


_Q_TILES = Q_W // PROJ_TN
_KV_TILES = (2 * KV_W) // PROJ_TN
_U_TILES = POOL_W // PROJ_TN
_G_TILES = (2 * D_MODEL) // PROJ_TN
_KV_T0 = _Q_TILES
_U_T0 = _KV_T0 + _KV_TILES
_G_T0 = _U_T0 + _U_TILES
_PROJ_TILES = _G_T0 + _G_TILES


def _proj_kernel(x_ref, mod_ref, g_ref, w_ref, b_ref, q_ref, kv_ref, u_ref, gl_ref, h_ref):
    j = pl.program_id(1)

    @pl.when(j == 0)
    def _():
        m = mod_ref[0]
        h = _rms(x_ref[...]) * g_ref[...] * (1.0 + m[1:2, :]) + m[0:1, :]
        h_ref[...] = h.astype(BF16)

    def project(o_ref, scale=None):
        for r0 in range(0, PROJ_TM, PROJ_SUB):
            y = jnp.dot(h_ref[r0:r0 + PROJ_SUB, :], w_ref[...],
                        preferred_element_type=F32) + b_ref[...]
            if scale is not None:
                y = y * scale
            o_ref[r0:r0 + PROJ_SUB, :] = y.astype(o_ref.dtype)

    @pl.when(j < _KV_T0)
    def _():
        project(q_ref, ATTN_SCALE)

    @pl.when((j >= _KV_T0) & (j < _U_T0))
    def _():
        project(kv_ref)

    @pl.when((j >= _U_T0) & (j < _G_T0))
    def _():
        project(u_ref)

    @pl.when(j >= _G_T0)
    def _():
        project(gl_ref)


def _in_projection(x2, mod6, g_pre, w_in, b_in, seq):
    t, d = x2.shape
    tiles_per_seq = seq // PROJ_TM

    def clampj(j, lo, n):
        return jnp.clip(j - lo, 0, n - 1)

    return pl.pallas_call(
        _proj_kernel,
        grid=(t // PROJ_TM, _PROJ_TILES),
        in_specs=[
            pl.BlockSpec((PROJ_TM, d), lambda i, j: (i, 0)),
            pl.BlockSpec((1, N_MOD, d), lambda i, j: (i // tiles_per_seq, 0, 0)),
            pl.BlockSpec((1, d), lambda i, j: (0, 0)),
            pl.BlockSpec((d, PROJ_TN), lambda i, j: (0, j)),
            pl.BlockSpec((1, PROJ_TN), lambda i, j: (0, j)),
        ],
        out_specs=[
            pl.BlockSpec((PROJ_TM, PROJ_TN), lambda i, j: (i, clampj(j, 0, _Q_TILES))),
            pl.BlockSpec((PROJ_TM, PROJ_TN), lambda i, j: (i, clampj(j, _KV_T0, _KV_TILES))),
            pl.BlockSpec((PROJ_TM, PROJ_TN), lambda i, j: (i, clampj(j, _U_T0, _U_TILES))),
            pl.BlockSpec((PROJ_TM, PROJ_TN), lambda i, j: (i, clampj(j, _G_T0, _G_TILES))),
        ],
        out_shape=[
            jax.ShapeDtypeStruct((t, Q_W), BF16),
            jax.ShapeDtypeStruct((t, 2 * KV_W), BF16),
            jax.ShapeDtypeStruct((t, POOL_W), F32),
            jax.ShapeDtypeStruct((t, 2 * D_MODEL), F32),
        ],
        scratch_shapes=[pltpu.VMEM((PROJ_TM, d), BF16)],
        compiler_params=_cparams(("arbitrary", "arbitrary")),
        name="in_projection",
    )(x2, mod6, g_pre, w_in, b_in)


def _attn_kernel(sink_ref, q_ref, kvp_ref, kvc_ref, o_ref, *, blocks_per_seq):
    i = pl.program_id(0)
    nq = ATT_TQ // WINDOW
    rows = N_Q_HEADS * WINDOW
    hrows = GROUP * WINDOW
    r_iota = lax.broadcasted_iota(jnp.int32, (rows, 2 * WINDOW), 0)
    k_iota = lax.broadcasted_iota(jnp.int32, (rows, 2 * WINDOW), 1)
    qi = r_iota % WINDOW
    band = (k_iota > qi) & (k_iota <= qi + WINDOW)
    head_of_row = lax.broadcasted_iota(jnp.int32, (rows, 1), 0) // WINDOW
    sink = jnp.zeros((rows, 1), F32)
    for qh in range(N_Q_HEADS):
        sink = jnp.where(head_of_row == qh, sink_ref[qh], sink)

    for n in range(nq):
        if n == 0:
            first = (i % blocks_per_seq) == 0
            mask = band & ((k_iota >= WINDOW) | jnp.logical_not(first))
            kv_prev = kvp_ref[...]
        else:
            mask = band
            kv_prev = kvc_ref[(n - 1) * WINDOW:n * WINDOW, :]
        kv_cur = kvc_ref[n * WINDOW:(n + 1) * WINDOW, :]
        kv = jnp.concatenate([kv_prev, kv_cur], axis=0)
        qn = q_ref[n * WINDOW:(n + 1) * WINDOW, :]
        s = jnp.concatenate(
            [lax.dot_general(
                jnp.concatenate(
                    [qn[:, (h * GROUP + g) * HEAD_DIM:(h * GROUP + g + 1) * HEAD_DIM]
                     for g in range(GROUP)], axis=0),
                kv[:, h * HEAD_DIM:(h + 1) * HEAD_DIM],
                (((1,), (1,)), ((), ())), preferred_element_type=F32)
             for h in range(N_KV_HEADS)], axis=0)
        s = jnp.where(mask, s, NEG_INF)
        m = jnp.maximum(jnp.max(s, axis=-1, keepdims=True), sink)
        p = jnp.exp(s - m)
        inv = 1.0 / (jnp.sum(p, axis=-1, keepdims=True) + jnp.exp(sink - m))
        pb = p.astype(BF16)
        for h in range(N_KV_HEADS):
            v = kv[:, KV_W + h * HEAD_DIM:KV_W + (h + 1) * HEAD_DIM]
            o = jnp.dot(pb[h * hrows:(h + 1) * hrows, :], v, preferred_element_type=F32)
            o = o * inv[h * hrows:(h + 1) * hrows, :]
            for g in range(GROUP):
                c0 = (h * GROUP + g) * HEAD_DIM
                o_ref[n * WINDOW:(n + 1) * WINDOW, c0:c0 + HEAD_DIM] = (
                    o[g * WINDOW:(g + 1) * WINDOW, :].astype(BF16))


def _attention(q, kv, sinks, seq):
    t = q.shape[0]
    blocks_per_seq = seq // ATT_TQ
    wpb = ATT_TQ // WINDOW
    return pl.pallas_call(
        functools.partial(_attn_kernel, blocks_per_seq=blocks_per_seq),
        grid=(t // ATT_TQ,),
        in_specs=[
            pl.BlockSpec(memory_space=pltpu.SMEM),
            pl.BlockSpec((ATT_TQ, Q_W), lambda i: (i, 0)),
            pl.BlockSpec((WINDOW, 2 * KV_W), lambda i: (jnp.maximum(i * wpb - 1, 0), 0)),
            pl.BlockSpec((ATT_TQ, 2 * KV_W), lambda i: (i, 0)),
        ],
        out_specs=pl.BlockSpec((ATT_TQ, Q_W), lambda i: (i, 0)),
        out_shape=jax.ShapeDtypeStruct((t, Q_W), BF16),
        compiler_params=_cparams(("arbitrary",)),
        name="swa_attention",
    )(sinks, q, kv, kv)


def _mix_kernel(x_ref, a_ref, up_ref, u_ref, gl_ref, mod_ref, wpm_ref, ps_ref, wa_ref, wp_ref,
                wo_ref, gpost_ref, gffn_ref, wrt_ref, brt_ref,
                x1_ref, h2_ref, idx_ref, gw_ref, rank_ref, cnt_ref, ubuf_ref, base_ref,
                *, tiles_per_seq):
    i = pl.program_id(0)
    tm = MIX_TM
    ne = N_EXPERTS
    seq_tile = i % tiles_per_seq
    m = mod_ref[0]

    halo = jnp.where(seq_tile == 0, 0.0, up_ref[...])
    ubuf_ref[0:POOL_HALO, :] = halo
    ubuf_ref[POOL_HALO:, :] = u_ref[...]
    ya = jnp.dot(a_ref[...], wa_ref[...], preferred_element_type=F32)
    pos = seq_tile * tm + lax.broadcasted_iota(jnp.int32, (tm, 1), 0)
    p_parts = []
    for g, w in enumerate(POOL_WINDOWS):
        c0 = g * POOL_GC
        acc = ubuf_ref[POOL_HALO:POOL_HALO + tm, c0:c0 + POOL_GC]
        ug = acc
        for dlt in range(1, w):
            acc = acc + ubuf_ref[POOL_HALO - dlt:POOL_HALO - dlt + tm, c0:c0 + POOL_GC]
        cnt = jnp.minimum(pos + 1, w).astype(F32)
        pg = acc / cnt - ug
        p_parts.append(jnp.dot(pg.astype(BF16), wpm_ref[g], preferred_element_type=F32))
    p = jnp.concatenate(p_parts, axis=-1) * ps_ref[...]

    yp =jnp.dot(p.astype(BF16), wp_ref[...], preferred_element_type=F32)
    gates_a = jax.nn.sigmoid(gl_ref[:, 0:D_MODEL])
    gates_p = jax.nn.sigmoid(gl_ref[:, D_MODEL:2 * D_MODEL])
    merged = gates_a * ya + gates_p * yp
    mix = jnp.dot(merged.astype(BF16), wo_ref[...], preferred_element_type=F32)
    x1 = x_ref[...] + m[2:3, :] * (_rms(mix) * gpost_ref[...])
    x1_ref[...] = x1

    h2 = _rms(x1) * gffn_ref[...] * (1.0 + m[4:5, :]) + m[3:4, :]
    h2_ref[...] = h2

    h_hi = h2.astype(BF16)
    h_lo = (h2 - h_hi.astype(F32)).astype(BF16)
    prod = lax.dot_general(wrt_ref[...], jnp.concatenate([h_hi, h_lo], axis=0),
                           (((1,), (1,)), ((), ())), preferred_element_type=F32)
    lg = ((prod[0:ne, 0:tm] + prod[ne:, 0:tm]) + (prod[0:ne, tm:] + prod[ne:, tm:])) + brt_ref[...]
    e_iota = lax.broadcasted_iota(jnp.int32, (ne, tm), 0)
    vals, idxs, picks = [], [], []
    for _ in range(TOP_K):
        mx = jnp.max(lg, axis=0, keepdims=True)
        ix = jnp.min(jnp.where(lg == mx, e_iota, ne), axis=0, keepdims=True)
        pick = e_iota == ix
        vals.append(mx)
        idxs.append(ix)
        picks.append(pick)
        lg = jnp.where(pick, -jnp.inf, lg)
    ex = [jnp.exp(v - vals[0]) for v in vals]
    den = ex[0] + ex[1] + ex[2] + ex[3]
    gw_ref[...] = jnp.concatenate([e / den for e in ex], axis=0)
    idx_ref[...] = jnp.concatenate(idxs, axis=0)

    @pl.when(i == 0)
    def _():
        base_ref[...] = jnp.zeros_like(base_ref)

    sel = (picks[0] | picks[1] | picks[2] | picks[3]).astype(F32)
    earlier = (lax.broadcasted_iota(jnp.int32, (tm, tm), 0)
               < lax.broadcasted_iota(jnp.int32, (tm, tm), 1)).astype(BF16)
    rank = jnp.dot(sel.astype(BF16), earlier, preferred_element_type=F32) + base_ref[...]
    rank_ref[...] = jnp.concatenate(
        [jnp.sum(jnp.where(pk, rank, 0.0), axis=0, keepdims=True) for pk in picks],
        axis=0).astype(jnp.int32)
    total = base_ref[...] + jnp.sum(sel, axis=1, keepdims=True)
    base_ref[...] = total
    cnt_ref[...] = total


def _split_bf16_rows(w):
    hi = w.astype(BF16)
    lo = (w - hi.astype(F32)).astype(BF16)
    return jnp.concatenate([hi, lo], axis=0)


def _mix(x2, a, u, gl, mod6, wpm, ps, wa, wp, wo, gpost, gffn, wrt, brt, seq):
    t, d = x2.shape
    tm = MIX_TM
    tiles_per_seq = seq // tm
    hb = tm // POOL_HALO
    const2 = lambda i: (0, 0)
    return pl.pallas_call(
        functools.partial(_mix_kernel, tiles_per_seq=tiles_per_seq),
        grid=(t // tm,),
        in_specs=[
            pl.BlockSpec((tm, d), lambda i: (i, 0)),
            pl.BlockSpec((tm, Q_W), lambda i: (i, 0)),
            pl.BlockSpec((POOL_HALO, POOL_W), lambda i: (jnp.maximum(i * hb - 1, 0), 0)),
            pl.BlockSpec((tm, POOL_W), lambda i: (i, 0)),
            pl.BlockSpec((tm, 2 * d), lambda i: (i, 0)),
            pl.BlockSpec((1, N_MOD, d), lambda i: (i // tiles_per_seq, 0, 0)),
            pl.BlockSpec(wpm.shape, lambda i: (0, 0, 0), pipeline_mode=pl.Buffered(1)),
            pl.BlockSpec((1, POOL_W), const2),
            pl.BlockSpec(wa.shape, const2, pipeline_mode=pl.Buffered(1)),
            pl.BlockSpec(wp.shape, const2, pipeline_mode=pl.Buffered(1)),
            pl.BlockSpec(wo.shape, const2, pipeline_mode=pl.Buffered(1)),
            pl.BlockSpec((1, d), const2),
            pl.BlockSpec((1, d), const2),
            pl.BlockSpec(wrt.shape, const2),
            pl.BlockSpec((N_EXPERTS, 1), const2),
        ],
        out_specs=[
            pl.BlockSpec((tm, d), lambda i: (i, 0)),
            pl.BlockSpec((tm, d), lambda i: (i, 0)),
            pl.BlockSpec((TOP_K, tm), lambda i: (0, i)),
            pl.BlockSpec((TOP_K, tm), lambda i: (0, i)),
            pl.BlockSpec((TOP_K, tm), lambda i: (0, i)),
            pl.BlockSpec((N_EXPERTS, 1), const2),
        ],
        out_shape=[
            jax.ShapeDtypeStruct((t, d), F32),
            jax.ShapeDtypeStruct((t, d), F32),
            jax.ShapeDtypeStruct((TOP_K, t), jnp.int32),
            jax.ShapeDtypeStruct((TOP_K, t), F32),
            jax.ShapeDtypeStruct((TOP_K, t), jnp.int32),
            jax.ShapeDtypeStruct((N_EXPERTS, 1), F32),
        ],
        scratch_shapes=[pltpu.VMEM((POOL_HALO + tm, POOL_W), F32),
                        pltpu.VMEM((N_EXPERTS, 1), F32)],
        compiler_params=_cparams(("arbitrary",)),
        name="mix_merge_router",
    )(x2, a, u, u, gl, mod6, wpm, ps, wa, wp, wo, gpost, gffn, wrt, brt)


def _dispatch_kernel(dest_ref, h_ref, xs_hbm, sem):
    n = DISP_TM * TOP_K

    tok_per_block = DMA_UNROLL // TOP_K

    def block(o, c):
        tok0 = pl.multiple_of(o * tok_per_block, tok_per_block)
        for j in range(DMA_UNROLL):
            pltpu.make_async_copy(
                h_ref.at[pl.ds(tok0 + j // TOP_K, 1)],
                xs_hbm.at[pl.ds(dest_ref[0, 0, o * DMA_UNROLL + j], 1)],
                sem).start(priority=j % DMA_PRIORITIES)
        return c

    lax.fori_loop(0, n // DMA_UNROLL, block, 0)
    pltpu.make_async_copy(xs_hbm.at[pl.ds(0, n)], xs_hbm.at[pl.ds(0, n)], sem).wait()


def _dispatch(dest3, h2, n_rows):
    t, d = h2.shape
    steps = t // DISP_TM
    return pl.pallas_call(
        _dispatch_kernel,
        grid=(steps,),
        in_specs=[
            pl.BlockSpec((1, 1, DISP_TM * TOP_K), lambda i: (i, 0, 0), memory_space=pltpu.SMEM),
            pl.BlockSpec((DISP_TM, d), lambda i: (i, 0)),
        ],
        out_specs=pl.BlockSpec(memory_space=pl.ANY),
        out_shape=jax.ShapeDtypeStruct((n_rows, d), F32),
        scratch_shapes=[pltpu.SemaphoreType.DMA(())],
        compiler_params=_cparams(("arbitrary",)),
        name="moe_dispatch",
    )(dest3, h2)


_MOE_NC = D_FF // MOE_FC
_GU_W = 2 * MOE_FC


def _moe_kernel(ie_ref, ir_ref, ins_ref, inv_ref, xs_hbm, w1_hbm, b1_ref, w2_hbm, b2_ref, y_hbm,
                xs_v, acc_v, stage_v, w1f_v, w2f_v, w1_v, w2i_v, w2_v, gu_v,
                in_sem, out_sem, w_sem):
    i = pl.program_id(0)

    @pl.when(ins_ref[i] > 0)
    def _():
        _moe_item(i, ie_ref, ir_ref, ins_ref, inv_ref, xs_hbm, w1_hbm, b1_ref, w2_hbm, b2_ref,
                  y_hbm, xs_v, acc_v, stage_v, w1f_v, w2f_v, w1_v, w2i_v, w2_v, gu_v,
                  in_sem, out_sem, w_sem)


def _moe_item(i, ie_ref, ir_ref, ins_ref, inv_ref, xs_hbm, w1_hbm, b1_ref, w2_hbm, b2_ref, y_hbm,
              xs_v, acc_v, stage_v, w1f_v, w2f_v, w1_v, w2i_v, w2_v, gu_v,
              in_sem, out_sem, w_sem):
    row0 = ir_ref[i]
    nsub = ins_ref[i]
    nvalid = inv_ref[i]
    has_next = ins_ref[i + 1] > 0
    r = MOE_R
    last_c = _MOE_NC - 1

    def rows(s):
        return pl.ds(pl.multiple_of(s * r, r), r)

    def in_copy(item, s):
        src0 = ir_ref[item] + s * r
        return pltpu.make_async_copy(
            xs_hbm.at[pl.ds(pl.multiple_of(src0, r), r)], stage_v.at[rows(s)], in_sem.at[s])

    def out_copy(s):
        return pltpu.make_async_copy(
            acc_v.at[rows(s)], y_hbm.at[pl.ds(pl.multiple_of(row0 + s * r, r), r)], out_sem)

    def weight_copies(item, c, slot):
        e = ie_ref[item]
        return (
            pltpu.make_async_copy(
                w1_hbm.at[e, :, pl.ds(pl.multiple_of(c * _GU_W, _GU_W), _GU_W)],
                w1f_v.at[slot], w_sem.at[0, slot]),
            pltpu.make_async_copy(
                w2_hbm.at[e, pl.ds(pl.multiple_of(c * MOE_FC, MOE_FC), MOE_FC), :],
                w2f_v.at[slot], w_sem.at[1, slot]),
        )

    def start_weights(item, c, slot):
        for cp in weight_copies(item, c, slot):
            cp.start()

    def wait_weights(c, slot):
        for cp in weight_copies(i, c, slot):
            cp.wait()

    def each(n, fn):
        def body(s, carry):
            fn(s)
            return carry
        lax.fori_loop(0, n, body, 0)

    def prep_w1(slot):
        w1_v[...] = w1f_v[slot].astype(BF16)

    def prep_w2(slot):
        half = LANES // 2
        for cb in range(D_MODEL // LANES):
            cs = slice(cb * LANES, (cb + 1) * LANES)
            for blk in range(MOE_FC // LANES):
                b0 = blk * LANES
                w2i_v[cb, pl.ds(b0, half, stride=2), :] = w2f_v[slot, b0:b0 + half, cs]
                w2i_v[cb, pl.ds(b0 + 1, half, stride=2), :] = w2f_v[slot, b0 + half:b0 + LANES, cs]
            w2_v[slot, :, cs] = w2i_v[cb].astype(BF16)

    lane = lax.broadcasted_iota(jnp.int32, (r, LANES), 1)
    even = (lane % 2) == 0

    def gate_up(s, c):
        b1 = b1_ref[0, :, pl.ds(pl.multiple_of(c * _GU_W, _GU_W), _GU_W)]
        return jnp.dot(xs_v[rows(s), :], w1_v[...], preferred_element_type=F32) + b1

    def activation(gu):
        glu = jnp.minimum(gu, SWIGLU_LIMIT)
        f = glu * jax.nn.sigmoid(SWIGLU_ALPHA * glu)
        lin = jnp.clip(gu, -SWIGLU_LIMIT, SWIGLU_LIMIT) + 1.0
        parts = []
        for blk in range(MOE_FC // LANES):
            ca = 2 * blk * LANES
            cb = ca + LANES
            za = f[:, ca:ca + LANES] * pltpu.roll(lin[:, ca:ca + LANES], LANES - 1, 1)
            zb = pltpu.roll(f[:, cb:cb + LANES], 1, 1) * lin[:, cb:cb + LANES]
            parts.append(jnp.where(even, za, zb))
        return jnp.concatenate(parts, axis=-1).astype(BF16)

    def load_rows(s):
        in_copy(i, s).wait()
        live = (s * r + lax.broadcasted_iota(jnp.int32, (r, 1), 0)) < nvalid
        xs_v[rows(s), :] = jnp.where(live, stage_v[rows(s), :], 0.0).astype(BF16)

    def down(s, gu, slot):
        return jnp.dot(activation(gu), w2_v[slot], preferred_element_type=F32)

    def pipelined_rows(c, slot, finish, first, tslot, after_first):
        g0 = gate_up(0, c)
        gu_v[0] = g0
        gu_v[tslot] = g0
        after_first()

        def step(s, g):
            if first:
                load_rows(s + 1)
            nxt = gate_up(s + 1, c)
            finish(s, gu_v[g], slot)
            gu_v[1 - g] = nxt
            gu_v[tslot] = nxt

        def pair(p, carry):
            step(2 * p, 0)
            step(2 * p + 1, 1)
            return carry

        def single(_, carry):
            step(nsub - 2, 0)
            return carry

        n_steps = nsub - 1
        lax.fori_loop(0, lax.shift_right_logical(n_steps, 1), pair, 0)
        lax.fori_loop(0, n_steps & 1, single, 0)

    def finish_first(s, gu, slot):
        acc_v[rows(s), :] = down(s, gu, slot)

    def finish_mid(s, gu, slot):
        acc_v[rows(s), :] = acc_v[rows(s), :] + down(s, gu, slot)

    def finish_last(s, gu, slot):
        acc_v[rows(s), :] = acc_v[rows(s), :] + down(s, gu, slot) + b2_ref[0]
        out_copy(s).start()

    tail = nsub - 1

    @pl.when(i == 0)
    def _():
        start_weights(i, 0, 0)
        each(nsub, lambda s: in_copy(i, s).start())

    wait_weights(0, 0)
    start_weights(i, 1, 1)

    @pl.when(i > 0)
    def _():
        each(ins_ref[jnp.maximum(i - 1, 0)], lambda s: out_copy(s).wait())

    prep_w1(0)
    load_rows(0)
    pipelined_rows(0, 0, finish_first, True, 2, lambda: prep_w2(0))

    @pl.when(has_next)
    def _():
        each(ins_ref[i + 1], lambda s: in_copy(i + 1, s).start())

    def mid_chunk(c, slot):
        wait_weights(c, slot)
        start_weights(i, c + 1, 1 - slot)
        prep_w1(slot)

        def after_first():
            prev = down(tail, gu_v[3 - slot], 1 - slot)
            acc_v[rows(tail), :] = jnp.where(c == 1, prev, acc_v[rows(tail), :] + prev)
            prep_w2(slot)

        pipelined_rows(c, slot, finish_mid, False, 2 + slot, after_first)

    def chunk_pair(p, carry):
        mid_chunk(2 * p + 1, 1)
        mid_chunk(2 * p + 2, 0)
        return carry

    lax.fori_loop(0, (last_c - 1) // 2, chunk_pair, 0)

    lslot = last_c & 1
    wait_weights(last_c, lslot)

    @pl.when(has_next)
    def _():
        start_weights(i + 1, 0, 1 - lslot)

    prep_w1(lslot)

    def last_after_first():
        finish_mid(tail, gu_v[3 - lslot], 1 - lslot)
        prep_w2(lslot)

    pipelined_rows(last_c, lslot, finish_last, False, 2 + lslot, last_after_first)
    finish_last(tail, gu_v[2 + lslot], lslot)

    @pl.when(jnp.logical_not(has_next))
    def _():
        each(nsub, lambda s: out_copy(s).wait())


def _moe(item_e, item_row0, item_nsub, item_nvalid, xs, w_gate_up, b_gate_up, w_down, b_down):
    n_items = item_e.shape[0]
    n_rows, d = xs.shape
    e = w_gate_up.shape[0]
    b1 = b_gate_up.reshape(e, 1, 2 * D_FF)
    b2 = b_down.reshape(e, 1, d)
    pad1 = lambda a: jnp.pad(a, (0, 1))

    grid_spec = pltpu.PrefetchScalarGridSpec(
        num_scalar_prefetch=4,
        grid=(n_items,),
        in_specs=[
            pl.BlockSpec(memory_space=pl.ANY),
            pl.BlockSpec(memory_space=pl.ANY),
            pl.BlockSpec((1, 1, 2 * D_FF), lambda i, ie, ir, ins, inv: (ie[i], 0, 0)),
            pl.BlockSpec(memory_space=pl.ANY),
            pl.BlockSpec((1, 1, d), lambda i, ie, ir, ins, inv: (ie[i], 0, 0)),
        ],
        out_specs=pl.BlockSpec(memory_space=pl.ANY),
        scratch_shapes=[
            pltpu.VMEM((MOE_TMAX, d), BF16),
            pltpu.VMEM((MOE_TMAX, d), F32),
            pltpu.VMEM((MOE_TMAX, d), F32),
            pltpu.VMEM((2, d, _GU_W), F32),
            pltpu.VMEM((2, MOE_FC, d), F32),
            pltpu.VMEM((d, _GU_W), BF16),
            pltpu.VMEM((d // LANES, MOE_FC, LANES), F32),
            pltpu.VMEM((2, MOE_FC, d), BF16),
            pltpu.VMEM((4, MOE_R, _GU_W), F32),
            pltpu.SemaphoreType.DMA((MOE_TMAX // MOE_R,)),
            pltpu.SemaphoreType.DMA(()),
            pltpu.SemaphoreType.DMA((2, 2)),
        ],
    )
    return pl.pallas_call(
        _moe_kernel,
        grid_spec=grid_spec,
        out_shape=jax.ShapeDtypeStruct((n_rows, d), F32),
        compiler_params=_cparams(("arbitrary",)),
        name="moe_experts",
    )(pad1(item_e), pad1(item_row0), pad1(item_nsub), pad1(item_nvalid),
      xs, w_gate_up, b1, w_down, b2)


def _combine_kernel(dest_ref, dest_next_ref, y_hbm, gw_ref, x1_ref, mod_ref, g_ref, o_ref,
                    buf, sem):
    i = pl.program_id(0)
    n = COMB_TM * TOP_K
    slot = i & 1

    blocks_per_k = COMB_TM // DMA_UNROLL

    def gather(d_ref, sl):
        def block(o, c):
            k = o if blocks_per_k == 1 else lax.div(o, blocks_per_k)
            tok0 = pl.multiple_of((o & (blocks_per_k - 1)) * DMA_UNROLL, DMA_UNROLL)
            for j in range(DMA_UNROLL):
                pltpu.make_async_copy(
                    y_hbm.at[pl.ds(d_ref[0, 0, o * DMA_UNROLL + j], 1)],
                    buf.at[sl, k, pl.ds(tok0 + j, 1)],
                    sem.at[sl]).start(priority=j % DMA_PRIORITIES)
            return c

        lax.fori_loop(0, n // DMA_UNROLL, block, 0)

    @pl.when(i == 0)
    def _():
        gather(dest_ref, 0)

    @pl.when(i + 1 < pl.num_programs(0))
    def _():
        gather(dest_next_ref, 1 - slot)

    pltpu.make_async_copy(buf.at[slot], buf.at[slot], sem.at[slot]).wait()

    gw = gw_ref[...]
    y = buf[slot, 0] * gw[:, 0:1]
    for k in range(1, TOP_K):
        y = y + buf[slot, k] * gw[:, k:k + 1]
    m = mod_ref[0]
    o_ref[...] = x1_ref[...] + m[5:6, :] * (_rms(y) * g_ref[...])


def _combine(dest3, yb, gw, x1, mod6, g_post, seq):
    t, d = x1.shape
    steps = t // COMB_TM
    tiles_per_seq = seq // COMB_TM
    return pl.pallas_call(
        _combine_kernel,
        grid=(steps,),
        in_specs=[
            pl.BlockSpec((1, 1, COMB_TM * TOP_K), lambda i: (i, 0, 0), memory_space=pltpu.SMEM),
            pl.BlockSpec((1, 1, COMB_TM * TOP_K), lambda i: (jnp.minimum(i + 1, steps - 1), 0, 0),
                         memory_space=pltpu.SMEM),
            pl.BlockSpec(memory_space=pl.ANY),
            pl.BlockSpec((COMB_TM, TOP_K), lambda i: (i, 0)),
            pl.BlockSpec((COMB_TM, d), lambda i: (i, 0)),
            pl.BlockSpec((1, N_MOD, d), lambda i: (i // tiles_per_seq, 0, 0)),
            pl.BlockSpec((1, d), lambda i: (0, 0)),
        ],
        out_specs=pl.BlockSpec((COMB_TM, d), lambda i: (i, 0)),
        out_shape=jax.ShapeDtypeStruct((t, d), F32),
        scratch_shapes=[pltpu.VMEM((2, TOP_K, COMB_TM, d), F32), pltpu.SemaphoreType.DMA((2,))],
        compiler_params=_cparams(("arbitrary",)),
        name="moe_combine",
    )(dest3, dest3, yb, gw, x1, mod6, g_post)


def _per_step_entries(dest_kt, tm):
    k, t = dest_kt.shape
    return dest_kt.reshape(k, t // tm, tm).transpose(1, 0, 2).reshape(t // tm, 1, k * tm)


def _routing_tables(top_idx, rank, counts, n_items):
    pcounts = (counts + MOE_R - 1) // MOE_R * MOE_R
    pend = jnp.cumsum(pcounts)
    pstart = pend - pcounts
    experts = jnp.arange(N_EXPERTS, dtype=jnp.int32)
    start_of = jnp.sum(jnp.where(top_idx[:, :, None] == experts, pstart, 0), axis=-1)
    dest = start_of + rank

    tiles = (pcounts + MOE_TMAX - 1) // MOE_TMAX
    tend = jnp.cumsum(tiles)
    tstart = tend - tiles
    it = jnp.arange(n_items, dtype=jnp.int32)
    ie = jnp.minimum(jnp.sum((tend[None, :] <= it[:, None]).astype(jnp.int32), axis=1),
                     N_EXPERTS - 1)
    valid = it < tend[-1]
    local = it - tstart[ie]
    row0 = pstart[ie] + local * MOE_TMAX
    nrows = jnp.clip(pcounts[ie] - local * MOE_TMAX, 0, MOE_TMAX)
    nsub = jnp.where(valid, nrows // MOE_R, 0).astype(jnp.int32)
    nvalid = jnp.where(valid, jnp.clip(counts[ie] - local * MOE_TMAX, 0, MOE_TMAX), 0)
    row0 = jnp.where(valid, row0, 0).astype(jnp.int32)
    return dest.astype(jnp.int32), ie, row0, nsub, nvalid.astype(jnp.int32)


def kernel(x, c, w_mod, b_mod, g_pre_mix, g_post_mix, w_in, b_in, attn_sinks, w_pool_mix,
           pool_scale, w_attn_branch, w_pool_branch, w_out, g_pre_ffn, g_post_ffn, w_router,
           b_router, w_gate_up, b_gate_up, w_down, b_down):
    b, s, d = x.shape
    t = b * s
    depth = w_mod.shape[0]
    n_assign = t * TOP_K
    n_items = N_EXPERTS + n_assign // MOE_TMAX
    n_rows = n_assign + N_EXPERTS * MOE_R

    x2 = x.reshape(t, d)
    c_pad = jnp.pad(c, ((0, SUBLANES - b), (0, 0)))
    for l in range(depth):
        mod = _modulation(c_pad, w_mod[l], b_mod[l][None, :])
        mod6 = mod[:b].reshape(b, N_MOD, d)

        q, kv, u, gl = _in_projection(x2, mod6, g_pre_mix[l][None, :], w_in[l].astype(BF16),
                                      b_in[l][None, :], s)
        a = _attention(q, kv, attn_sinks[l], s)
        x1, h2, top_idx, gate_w, rank, counts = _mix(
            x2, a, u, gl, mod6, w_pool_mix[l].astype(BF16), pool_scale[l][None, :],
            w_attn_branch[l].astype(BF16), w_pool_branch[l].astype(BF16),
            w_out[l].astype(BF16), g_post_mix[l][None, :], g_pre_ffn[l][None, :],
            _split_bf16_rows(w_router[l].T), b_router[l][:, None], s)

        dest, item_e, item_row0, item_nsub, item_nvalid = _routing_tables(
            top_idx, rank, counts[:, 0].astype(jnp.int32), n_items)
        xs = _dispatch(dest.T.reshape(t // DISP_TM, 1, DISP_TM * TOP_K), h2, n_rows)
        yb = _moe(item_e, item_row0, item_nsub, item_nvalid, xs, w_gate_up[l], b_gate_up[l],
                  w_down[l], b_down[l])
        x2 = _combine(_per_step_entries(dest, COMB_TM), yb, gate_w.T, x1, mod6,
                      g_post_ffn[l][None, :], s)
    return x2.reshape(b, s, d)
```

```python
import functools

import jax
import jax.numpy as jnp
from jax import lax
from jax.experimental import pallas as pl
from jax.experimental.pallas import tpu as pltpu

F32 = jnp.float32
BF16 = jnp.bfloat16

D_MODEL = 2048
HEAD_DIM = 64
N_Q_HEADS = 16
N_KV_HEADS = 4
GROUP = N_Q_HEADS // N_KV_HEADS
WINDOW = 128
ATTN_SCALE = HEAD_DIM ** -0.5
assert ATTN_SCALE == 0.125
Q_W = N_Q_HEADS * HEAD_DIM
KV_W = N_KV_HEADS * HEAD_DIM
POOL_WINDOWS = (2, 4, 8, 16)
POOL_W = D_MODEL // 2
POOL_GC = POOL_W // len(POOL_WINDOWS)
POOL_HALO = 16
assert all(w & (w - 1) == 0 and 2 <= w <= POOL_HALO for w in POOL_WINDOWS)
N_EXPERTS = 32
TOP_K = 4
_LOG2_TOP_K = 2
D_FF = D_MODEL
SWIGLU_LIMIT = 7.0
SWIGLU_ALPHA = 1.702
N_MOD = 6
RMS_EPS = 1e-6
NEG_INF = -1e30

LANES = 128
SUBLANES = 8
VMEM_LIMIT_BYTES = 56 * 1024 * 1024

MOD_TN = 1024
PROJ_TM = 1024
PROJ_TN = 512
PROJ_SUB = 1024
ATT_TQ = 512
MIX_TM = 256
MOE_R = 256
MOE_TMAX = 1536
MOE_FC = 256
DISP_TM = 1024
COMB_TM = 128
DMA_UNROLL = 128
DMA_PRIORITIES = 2


def _cparams(sem):
    return pltpu.CompilerParams(dimension_semantics=sem, vmem_limit_bytes=VMEM_LIMIT_BYTES)


def _rms(x):
    return x * lax.rsqrt(jnp.mean(x * x, axis=-1, keepdims=True) + RMS_EPS)


def _mod_kernel(c_ref, w_ref, b_ref, o_ref):
    c = c_ref[...]
    s = c * jax.nn.sigmoid(c)
    o_ref[...] = jnp.dot(s.astype(BF16), w_ref[...].astype(BF16),
                         preferred_element_type=F32) + b_ref[...]


def _modulation(c_pad, w_mod, b_mod):
    rows, d = c_pad.shape
    n = w_mod.shape[1]
    return pl.pallas_call(
        _mod_kernel,
        grid=(n // MOD_TN,),
        in_specs=[
            pl.BlockSpec((rows, d), lambda j: (0, 0)),
            pl.BlockSpec((d, MOD_TN), lambda j: (0, j)),
            pl.BlockSpec((1, MOD_TN), lambda j: (0, j)),
        ],
        out_specs=pl.BlockSpec((rows, MOD_TN), lambda j: (0, j)),
        out_shape=jax.ShapeDtypeStruct((rows, n), F32),
        compiler_params=_cparams(("arbitrary",)),
        name="modulation",
    )(c_pad, w_mod, b_mod)


_Q_TILES = Q_W // PROJ_TN
_KV_TILES = (2 * KV_W) // PROJ_TN
_U_TILES = POOL_W // PROJ_TN
_G_TILES = (2 * D_MODEL) // PROJ_TN
_KV_T0 = _Q_TILES
_U_T0 = _KV_T0 + _KV_TILES
_G_T0 = _U_T0 + _U_TILES
_PROJ_TILES = _G_T0 + _G_TILES


def _proj_kernel(x_ref, mod_ref, g_ref, w_ref, b_ref, q_ref, kv_ref, u_ref, gl_ref, h_ref):
    j = pl.program_id(1)

    @pl.when(j == 0)
    def _():
        m = mod_ref[0]
        h = _rms(x_ref[...]) * g_ref[...] * (1.0 + m[1:2, :]) + m[0:1, :]
        h_ref[...] = h.astype(BF16)

    def project(o_ref, scale=None):
        for r0 in range(0, PROJ_TM, PROJ_SUB):
            y = jnp.dot(h_ref[r0:r0 + PROJ_SUB, :], w_ref[...],
                        preferred_element_type=F32) + b_ref[...]
            if scale is not None:
                y = y * scale
            o_ref[r0:r0 + PROJ_SUB, :] = y.astype(o_ref.dtype)

    @pl.when(j < _KV_T0)
    def _():
        project(q_ref, ATTN_SCALE)

    @pl.when((j >= _KV_T0) & (j < _U_T0))
    def _():
        project(kv_ref)

    @pl.when((j >= _U_T0) & (j < _G_T0))
    def _():
        project(u_ref)

    @pl.when(j >= _G_T0)
    def _():
        project(gl_ref)


def _in_projection(x2, mod6, g_pre, w_in, b_in, seq):
    t, d = x2.shape
    tiles_per_seq = seq // PROJ_TM

    def clampj(j, lo, n):
        return jnp.clip(j - lo, 0, n - 1)

    return pl.pallas_call(
        _proj_kernel,
        grid=(t // PROJ_TM, _PROJ_TILES),
        in_specs=[
            pl.BlockSpec((PROJ_TM, d), lambda i, j: (i, 0)),
            pl.BlockSpec((1, N_MOD, d), lambda i, j: (i // tiles_per_seq, 0, 0)),
            pl.BlockSpec((1, d), lambda i, j: (0, 0)),
            pl.BlockSpec((d, PROJ_TN), lambda i, j: (0, j)),
            pl.BlockSpec((1, PROJ_TN), lambda i, j: (0, j)),
        ],
        out_specs=[
            pl.BlockSpec((PROJ_TM, PROJ_TN), lambda i, j: (i, clampj(j, 0, _Q_TILES))),
            pl.BlockSpec((PROJ_TM, PROJ_TN), lambda i, j: (i, clampj(j, _KV_T0, _KV_TILES))),
            pl.BlockSpec((PROJ_TM, PROJ_TN), lambda i, j: (i, clampj(j, _U_T0, _U_TILES))),
            pl.BlockSpec((PROJ_TM, PROJ_TN), lambda i, j: (i, clampj(j, _G_T0, _G_TILES))),
        ],
        out_shape=[
            jax.ShapeDtypeStruct((t, Q_W), BF16),
            jax.ShapeDtypeStruct((t, 2 * KV_W), BF16),
            jax.ShapeDtypeStruct((t, POOL_W), F32),
            jax.ShapeDtypeStruct((t, 2 * D_MODEL), F32),
        ],
        scratch_shapes=[pltpu.VMEM((PROJ_TM, d), BF16)],
        compiler_params=_cparams(("arbitrary", "arbitrary")),
        name="in_projection",
    )(x2, mod6, g_pre, w_in, b_in)


def _attn_kernel(sink_ref, q_ref, kvp_ref, kvc_ref, o_ref, *, blocks_per_seq):
    i = pl.program_id(0)
    nq = ATT_TQ // WINDOW
    rows = N_Q_HEADS * WINDOW
    hrows = GROUP * WINDOW
    r_iota = lax.broadcasted_iota(jnp.int32, (rows, 2 * WINDOW), 0)
    k_iota = lax.broadcasted_iota(jnp.int32, (rows, 2 * WINDOW), 1)
    qi = r_iota % WINDOW
    band = (k_iota > qi) & (k_iota <= qi + WINDOW)
    head_of_row = lax.broadcasted_iota(jnp.int32, (rows, 1), 0) // WINDOW
    sink = jnp.zeros((rows, 1), F32)
    for qh in range(N_Q_HEADS):
        sink = jnp.where(head_of_row == qh, sink_ref[qh], sink)

    for n in range(nq):
        if n == 0:
            first = (i % blocks_per_seq) == 0
            mask = band & ((k_iota >= WINDOW) | jnp.logical_not(first))
            kv_prev = kvp_ref[...]
        else:
            mask = band
            kv_prev = kvc_ref[(n - 1) * WINDOW:n * WINDOW, :]
        kv_cur = kvc_ref[n * WINDOW:(n + 1) * WINDOW, :]
        kv = jnp.concatenate([kv_prev, kv_cur], axis=0)
        qn = q_ref[n * WINDOW:(n + 1) * WINDOW, :]
        s = jnp.concatenate(
            [lax.dot_general(
                jnp.concatenate(
                    [qn[:, (h * GROUP + g) * HEAD_DIM:(h * GROUP + g + 1) * HEAD_DIM]
                     for g in range(GROUP)], axis=0),
                kv[:, h * HEAD_DIM:(h + 1) * HEAD_DIM],
                (((1,), (1,)), ((), ())), preferred_element_type=F32)
             for h in range(N_KV_HEADS)], axis=0)
        s = jnp.where(mask, s, NEG_INF)
        m = jnp.maximum(jnp.max(s, axis=-1, keepdims=True), sink)
        p = jnp.exp(s - m)
        inv = 1.0 / (jnp.sum(p, axis=-1, keepdims=True) + jnp.exp(sink - m))
        pb = p.astype(BF16)
        for h in range(N_KV_HEADS):
            v = kv[:, KV_W + h * HEAD_DIM:KV_W + (h + 1) * HEAD_DIM]
            o = jnp.dot(pb[h * hrows:(h + 1) * hrows, :], v, preferred_element_type=F32)
            o = o * inv[h * hrows:(h + 1) * hrows, :]
            for g in range(GROUP):
                c0 = (h * GROUP + g) * HEAD_DIM
                o_ref[n * WINDOW:(n + 1) * WINDOW, c0:c0 + HEAD_DIM] = (
                    o[g * WINDOW:(g + 1) * WINDOW, :].astype(BF16))


def _attention(q, kv, sinks, seq):
    t = q.shape[0]
    blocks_per_seq = seq // ATT_TQ
    wpb = ATT_TQ // WINDOW
    return pl.pallas_call(
        functools.partial(_attn_kernel, blocks_per_seq=blocks_per_seq),
        grid=(t // ATT_TQ,),
        in_specs=[
            pl.BlockSpec(memory_space=pltpu.SMEM),
            pl.BlockSpec((ATT_TQ, Q_W), lambda i: (i, 0)),
            pl.BlockSpec((WINDOW, 2 * KV_W), lambda i: (jnp.maximum(i * wpb - 1, 0), 0)),
            pl.BlockSpec((ATT_TQ, 2 * KV_W), lambda i: (i, 0)),
        ],
        out_specs=pl.BlockSpec((ATT_TQ, Q_W), lambda i: (i, 0)),
        out_shape=jax.ShapeDtypeStruct((t, Q_W), BF16),
        compiler_params=_cparams(("arbitrary",)),
        name="swa_attention",
    )(sinks, q, kv, kv)


def _mix_kernel(x_ref, a_ref, up_ref, u_ref, gl_ref, mod_ref, wpm_ref, ps_ref, wa_ref, wp_ref,
                wo_ref, gpost_ref, gffn_ref, wrt_ref, brt_ref,
                x1_ref, h2_ref, idx_ref, gw_ref, rank_ref, cnt_ref, ubuf_ref, pool_tmp, base_ref,
                *, tiles_per_seq):
    i = pl.program_id(0)
    tm = MIX_TM
    ne = N_EXPERTS
    seq_tile = i % tiles_per_seq
    m = mod_ref[0]

    pad = SUBLANES
    top = pad + POOL_HALO
    halo = jnp.where(seq_tile == 0, 0.0, up_ref[...])
    ubuf_ref[0:pad, :] = jnp.zeros((pad, POOL_W), F32)
    ubuf_ref[pad:top, :] = halo
    ubuf_ref[top:, :] = u_ref[...]
    pool_tmp[:, 0:pad, :] = jnp.zeros((2, pad, POOL_GC), F32)
    ya = jnp.dot(a_ref[...], wa_ref[...], preferred_element_type=F32)
    pos = seq_tile * tm + lax.broadcasted_iota(jnp.int32, (tm, 1), 0)
    p_parts = []
    for g, w in enumerate(POOL_WINDOWS):
        c0 = g * POOL_GC
        ug = ubuf_ref[top:top + tm, c0:c0 + POOL_GC]
        n_stage = w.bit_length() - 1
        read = lambda lo, hi: ubuf_ref[lo:hi, c0:c0 + POOL_GC]
        for j in range(1, n_stage + 1):
            d = 1 << (j - 1)
            if j < n_stage:
                pool_tmp[j % 2, pad:top + tm, :] = read(pad, top + tm) + read(pad - d, top + tm - d)
                read = functools.partial(lambda lo, hi, sl: pool_tmp[sl, lo:hi, :], sl=j % 2)
            else:
                acc = read(top, top + tm) + read(top - d, top + tm - d)
        cnt = jnp.minimum(pos + 1, w).astype(F32)
        pg = acc / cnt - ug
        p_parts.append(jnp.dot(pg.astype(BF16), wpm_ref[g], preferred_element_type=F32))
    p = jnp.concatenate(p_parts, axis=-1) * ps_ref[...]

    yp =jnp.dot(p.astype(BF16), wp_ref[...], preferred_element_type=F32)
    gates_a = jax.nn.sigmoid(gl_ref[:, 0:D_MODEL])
    gates_p = jax.nn.sigmoid(gl_ref[:, D_MODEL:2 * D_MODEL])
    merged = gates_a * ya + gates_p * yp
    mix = jnp.dot(merged.astype(BF16), wo_ref[...], preferred_element_type=F32)
    x1 = x_ref[...] + m[2:3, :] * (_rms(mix) * gpost_ref[...])
    x1_ref[...] = x1

    h2 = _rms(x1) * gffn_ref[...] * (1.0 + m[4:5, :]) + m[3:4, :]
    h2_ref[...] = h2

    h_hi = h2.astype(BF16)
    h_lo = (h2 - h_hi.astype(F32)).astype(BF16)
    prod = lax.dot_general(wrt_ref[...], jnp.concatenate([h_hi, h_lo], axis=0),
                           (((1,), (1,)), ((), ())), preferred_element_type=F32)
    lg = ((prod[0:ne, 0:tm] + prod[ne:, 0:tm]) + (prod[0:ne, tm:] + prod[ne:, tm:])) + brt_ref[...]
    e_iota = lax.broadcasted_iota(jnp.int32, (ne, tm), 0)
    vals, idxs, picks = [], [], []
    for _ in range(TOP_K):
        mx = jnp.max(lg, axis=0, keepdims=True)
        ix = jnp.min(jnp.where(lg == mx, e_iota, ne), axis=0, keepdims=True)
        pick = e_iota == ix
        vals.append(mx)
        idxs.append(ix)
        picks.append(pick)
        lg = jnp.where(pick, -jnp.inf, lg)
    ex = [jnp.exp(v - vals[0]) for v in vals]
    den = ex[0] + ex[1] + ex[2] + ex[3]
    gw_ref[...] = jnp.concatenate([e / den for e in ex], axis=0)
    idx_ref[...] = jnp.concatenate(idxs, axis=0)

    @pl.when(i == 0)
    def _():
        base_ref[...] = jnp.zeros_like(base_ref)

    sel = (picks[0] | picks[1] | picks[2] | picks[3]).astype(F32)
    earlier = (lax.broadcasted_iota(jnp.int32, (tm, tm), 0)
               < lax.broadcasted_iota(jnp.int32, (tm, tm), 1)).astype(BF16)
    rank = jnp.dot(sel.astype(BF16), earlier, preferred_element_type=F32) + base_ref[...]
    rank_ref[...] = jnp.concatenate(
        [jnp.sum(jnp.where(pk, rank, 0.0), axis=0, keepdims=True) for pk in picks],
        axis=0).astype(jnp.int32)
    total = base_ref[...] + jnp.sum(sel, axis=1, keepdims=True)
    base_ref[...] = total
    cnt_ref[...] = total


def _split_bf16_rows(w):
    hi = w.astype(BF16)
    lo = (w - hi.astype(F32)).astype(BF16)
    return jnp.concatenate([hi, lo], axis=0)


def _mix(x2, a, u, gl, mod6, wpm, ps, wa, wp, wo, gpost, gffn, wrt, brt, seq):
    t, d = x2.shape
    tm = MIX_TM
    tiles_per_seq = seq // tm
    hb = tm // POOL_HALO
    const2 = lambda i: (0, 0)
    return pl.pallas_call(
        functools.partial(_mix_kernel, tiles_per_seq=tiles_per_seq),
        grid=(t // tm,),
        in_specs=[
            pl.BlockSpec((tm, d), lambda i: (i, 0)),
            pl.BlockSpec((tm, Q_W), lambda i: (i, 0)),
            pl.BlockSpec((POOL_HALO, POOL_W), lambda i: (jnp.maximum(i * hb - 1, 0), 0)),
            pl.BlockSpec((tm, POOL_W), lambda i: (i, 0)),
            pl.BlockSpec((tm, 2 * d), lambda i: (i, 0)),
            pl.BlockSpec((1, N_MOD, d), lambda i: (i // tiles_per_seq, 0, 0)),
            pl.BlockSpec(wpm.shape, lambda i: (0, 0, 0), pipeline_mode=pl.Buffered(1)),
            pl.BlockSpec((1, POOL_W), const2),
            pl.BlockSpec(wa.shape, const2, pipeline_mode=pl.Buffered(1)),
            pl.BlockSpec(wp.shape, const2, pipeline_mode=pl.Buffered(1)),
            pl.BlockSpec(wo.shape, const2, pipeline_mode=pl.Buffered(1)),
            pl.BlockSpec((1, d), const2),
            pl.BlockSpec((1, d), const2),
            pl.BlockSpec(wrt.shape, const2),
            pl.BlockSpec((N_EXPERTS, 1), const2),
        ],
        out_specs=[
            pl.BlockSpec((tm, d), lambda i: (i, 0)),
            pl.BlockSpec((tm, d), lambda i: (i, 0)),
            pl.BlockSpec((TOP_K, tm), lambda i: (0, i)),
            pl.BlockSpec((TOP_K, tm), lambda i: (0, i)),
            pl.BlockSpec((TOP_K, tm), lambda i: (0, i)),
            pl.BlockSpec((N_EXPERTS, 1), const2),
        ],
        out_shape=[
            jax.ShapeDtypeStruct((t, d), F32),
            jax.ShapeDtypeStruct((t, d), F32),
            jax.ShapeDtypeStruct((TOP_K, t), jnp.int32),
            jax.ShapeDtypeStruct((TOP_K, t), F32),
            jax.ShapeDtypeStruct((TOP_K, t), jnp.int32),
            jax.ShapeDtypeStruct((N_EXPERTS, 1), F32),
        ],
        scratch_shapes=[pltpu.VMEM((SUBLANES + POOL_HALO + tm, POOL_W), F32),
                        pltpu.VMEM((2, SUBLANES + POOL_HALO + tm, POOL_GC), F32),
                        pltpu.VMEM((N_EXPERTS, 1), F32)],
        compiler_params=_cparams(("arbitrary",)),
        name="mix_merge_router",
    )(x2, a, u, u, gl, mod6, wpm, ps, wa, wp, wo, gpost, gffn, wrt, brt)


def _dispatch_kernel(dest_ref, h_ref, xs_hbm, sem):
    n = DISP_TM * TOP_K

    tok_per_block = DMA_UNROLL // TOP_K

    def block(o, c):
        tok0 = pl.multiple_of(o * tok_per_block, tok_per_block)
        for j in range(DMA_UNROLL):
            pltpu.make_async_copy(
                h_ref.at[pl.ds(tok0 + j // TOP_K, 1)],
                xs_hbm.at[pl.ds(dest_ref[0, 0, o * DMA_UNROLL + j], 1)],
                sem).start(priority=j % DMA_PRIORITIES)
        return c

    lax.fori_loop(0, n // DMA_UNROLL, block, 0)
    pltpu.make_async_copy(xs_hbm.at[pl.ds(0, n)], xs_hbm.at[pl.ds(0, n)], sem).wait()


def _dispatch(dest3, h2, n_rows):
    t, d = h2.shape
    steps = t // DISP_TM
    return pl.pallas_call(
        _dispatch_kernel,
        grid=(steps,),
        in_specs=[
            pl.BlockSpec((1, 1, DISP_TM * TOP_K), lambda i: (i, 0, 0), memory_space=pltpu.SMEM),
            pl.BlockSpec((DISP_TM, d), lambda i: (i, 0)),
        ],
        out_specs=pl.BlockSpec(memory_space=pl.ANY),
        out_shape=jax.ShapeDtypeStruct((n_rows, d), F32),
        scratch_shapes=[pltpu.SemaphoreType.DMA(())],
        compiler_params=_cparams(("arbitrary",)),
        name="moe_dispatch",
    )(dest3, h2)


_MOE_NC = D_FF // MOE_FC
_GU_W = 2 * MOE_FC


def _moe_kernel(ie_ref, ir_ref, ins_ref, inv_ref, xs_hbm, w1_hbm, b1_ref, w2_hbm, b2_ref, y_hbm,
                xs_v, acc_v, stage_v, w1f_v, w2f_v, w1_v, w2i_v, w2_v, gu_v,
                in_sem, out_sem, w_sem):
    i = pl.program_id(0)

    @pl.when(ins_ref[i] > 0)
    def _():
        _moe_item(i, ie_ref, ir_ref, ins_ref, inv_ref, xs_hbm, w1_hbm, b1_ref, w2_hbm, b2_ref,
                  y_hbm, xs_v, acc_v, stage_v, w1f_v, w2f_v, w1_v, w2i_v, w2_v, gu_v,
                  in_sem, out_sem, w_sem)


def _moe_item(i, ie_ref, ir_ref, ins_ref, inv_ref, xs_hbm, w1_hbm, b1_ref, w2_hbm, b2_ref, y_hbm,
              xs_v, acc_v, stage_v, w1f_v, w2f_v, w1_v, w2i_v, w2_v, gu_v,
              in_sem, out_sem, w_sem):
    row0 = ir_ref[i]
    nsub = ins_ref[i]
    nvalid = inv_ref[i]
    has_next = ins_ref[i + 1] > 0
    r = MOE_R
    last_c = _MOE_NC - 1

    def rows(s):
        return pl.ds(pl.multiple_of(s * r, r), r)

    def in_copy(item, s):
        src0 = ir_ref[item] + s * r
        return pltpu.make_async_copy(
            xs_hbm.at[pl.ds(pl.multiple_of(src0, r), r)], stage_v.at[rows(s)], in_sem.at[s])

    def out_copy(s):
        return pltpu.make_async_copy(
            acc_v.at[rows(s)], y_hbm.at[pl.ds(pl.multiple_of(row0 + s * r, r), r)], out_sem)

    def weight_copies(item, c, slot):
        e = ie_ref[item]
        return (
            pltpu.make_async_copy(
                w1_hbm.at[e, :, pl.ds(pl.multiple_of(c * _GU_W, _GU_W), _GU_W)],
                w1f_v.at[slot], w_sem.at[0, slot]),
            pltpu.make_async_copy(
                w2_hbm.at[e, pl.ds(pl.multiple_of(c * MOE_FC, MOE_FC), MOE_FC), :],
                w2f_v.at[slot], w_sem.at[1, slot]),
        )

    def start_weights(item, c, slot):
        for cp in weight_copies(item, c, slot):
            cp.start()

    def wait_weights(c, slot):
        for cp in weight_copies(i, c, slot):
            cp.wait()

    def each(n, fn):
        def body(s, carry):
            fn(s)
            return carry
        lax.fori_loop(0, n, body, 0)

    def prep_w1(slot):
        w1_v[...] = w1f_v[slot].astype(BF16)

    def prep_w2(slot):
        half = LANES // 2
        for cb in range(D_MODEL // LANES):
            cs = slice(cb * LANES, (cb + 1) * LANES)
            for blk in range(MOE_FC // LANES):
                b0 = blk * LANES
                w2i_v[cb, pl.ds(b0, half, stride=2), :] = w2f_v[slot, b0:b0 + half, cs]
                w2i_v[cb, pl.ds(b0 + 1, half, stride=2), :] = w2f_v[slot, b0 + half:b0 + LANES, cs]
            w2_v[slot, :, cs] = w2i_v[cb].astype(BF16)

    lane = lax.broadcasted_iota(jnp.int32, (r, LANES), 1)
    even = (lane % 2) == 0

    def gate_up(s, c):
        b1 = b1_ref[0, :, pl.ds(pl.multiple_of(c * _GU_W, _GU_W), _GU_W)]
        return jnp.dot(xs_v[rows(s), :], w1_v[...], preferred_element_type=F32) + b1

    def activation(gu):
        glu = jnp.minimum(gu, SWIGLU_LIMIT)
        f = glu * jax.nn.sigmoid(SWIGLU_ALPHA * glu)
        lin = jnp.clip(gu, -SWIGLU_LIMIT, SWIGLU_LIMIT) + 1.0
        parts = []
        for blk in range(MOE_FC // LANES):
            ca = 2 * blk * LANES
            cb = ca + LANES
            za = f[:, ca:ca + LANES] * pltpu.roll(lin[:, ca:ca + LANES], LANES - 1, 1)
            zb = pltpu.roll(f[:, cb:cb + LANES], 1, 1) * lin[:, cb:cb + LANES]
            parts.append(jnp.where(even, za, zb))
        return jnp.concatenate(parts, axis=-1).astype(BF16)

    def load_rows(s):
        in_copy(i, s).wait()
        live = (s * r + lax.broadcasted_iota(jnp.int32, (r, 1), 0)) < nvalid
        xs_v[rows(s), :] = jnp.where(live, stage_v[rows(s), :], 0.0).astype(BF16)

    def down(s, gu, slot):
        return jnp.dot(activation(gu), w2_v[slot], preferred_element_type=F32)

    def pipelined_rows(c, slot, finish, first, tslot, after_first):
        g0 = gate_up(0, c)
        gu_v[0] = g0
        gu_v[tslot] = g0
        after_first()

        def step(s, g):
            if first:
                load_rows(s + 1)
            nxt = gate_up(s + 1, c)
            finish(s, gu_v[g], slot)
            gu_v[1 - g] = nxt
            gu_v[tslot] = nxt

        def pair(p, carry):
            step(2 * p, 0)
            step(2 * p + 1, 1)
            return carry

        def single(_, carry):
            step(nsub - 2, 0)
            return carry

        n_steps = nsub - 1
        lax.fori_loop(0, lax.shift_right_logical(n_steps, 1), pair, 0)
        lax.fori_loop(0, n_steps & 1, single, 0)

    def finish_first(s, gu, slot):
        acc_v[rows(s), :] = down(s, gu, slot)

    def finish_mid(s, gu, slot):
        acc_v[rows(s), :] = acc_v[rows(s), :] + down(s, gu, slot)

    def finish_last(s, gu, slot):
        acc_v[rows(s), :] = acc_v[rows(s), :] + down(s, gu, slot) + b2_ref[0]
        out_copy(s).start()

    tail = nsub - 1

    @pl.when(i == 0)
    def _():
        start_weights(i, 0, 0)
        each(nsub, lambda s: in_copy(i, s).start())

    wait_weights(0, 0)
    start_weights(i, 1, 1)

    @pl.when(i > 0)
    def _():
        each(ins_ref[jnp.maximum(i - 1, 0)], lambda s: out_copy(s).wait())

    prep_w1(0)
    load_rows(0)
    pipelined_rows(0, 0, finish_first, True, 2, lambda: prep_w2(0))

    @pl.when(has_next)
    def _():
        each(ins_ref[i + 1], lambda s: in_copy(i + 1, s).start())

    def mid_chunk(c, slot):
        wait_weights(c, slot)
        start_weights(i, c + 1, 1 - slot)
        prep_w1(slot)

        def after_first():
            prev = down(tail, gu_v[3 - slot], 1 - slot)
            acc_v[rows(tail), :] = jnp.where(c == 1, prev, acc_v[rows(tail), :] + prev)
            prep_w2(slot)

        pipelined_rows(c, slot, finish_mid, False, 2 + slot, after_first)

    def chunk_pair(p, carry):
        mid_chunk(2 * p + 1, 1)
        mid_chunk(2 * p + 2, 0)
        return carry

    lax.fori_loop(0, (last_c - 1) // 2, chunk_pair, 0)

    lslot = last_c & 1
    wait_weights(last_c, lslot)

    @pl.when(has_next)
    def _():
        start_weights(i + 1, 0, 1 - lslot)

    prep_w1(lslot)

    def last_after_first():
        finish_mid(tail, gu_v[3 - lslot], 1 - lslot)
        prep_w2(lslot)

    pipelined_rows(last_c, lslot, finish_last, False, 2 + lslot, last_after_first)
    finish_last(tail, gu_v[2 + lslot], lslot)

    @pl.when(jnp.logical_not(has_next))
    def _():
        each(nsub, lambda s: out_copy(s).wait())


def _moe(item_e, item_row0, item_nsub, item_nvalid, xs, w_gate_up, b_gate_up, w_down, b_down):
    n_items = item_e.shape[0]
    n_rows, d = xs.shape
    e = w_gate_up.shape[0]
    b1 = b_gate_up.reshape(e, 1, 2 * D_FF)
    b2 = b_down.reshape(e, 1, d)
    pad1 = lambda a: jnp.pad(a, (0, 1))

    grid_spec = pltpu.PrefetchScalarGridSpec(
        num_scalar_prefetch=4,
        grid=(n_items,),
        in_specs=[
            pl.BlockSpec(memory_space=pl.ANY),
            pl.BlockSpec(memory_space=pl.ANY),
            pl.BlockSpec((1, 1, 2 * D_FF), lambda i, ie, ir, ins, inv: (ie[i], 0, 0)),
            pl.BlockSpec(memory_space=pl.ANY),
            pl.BlockSpec((1, 1, d), lambda i, ie, ir, ins, inv: (ie[i], 0, 0)),
        ],
        out_specs=pl.BlockSpec(memory_space=pl.ANY),
        scratch_shapes=[
            pltpu.VMEM((MOE_TMAX, d), BF16),
            pltpu.VMEM((MOE_TMAX, d), F32),
            pltpu.VMEM((MOE_TMAX, d), F32),
            pltpu.VMEM((2, d, _GU_W), F32),
            pltpu.VMEM((2, MOE_FC, d), F32),
            pltpu.VMEM((d, _GU_W), BF16),
            pltpu.VMEM((d // LANES, MOE_FC, LANES), F32),
            pltpu.VMEM((2, MOE_FC, d), BF16),
            pltpu.VMEM((4, MOE_R, _GU_W), F32),
            pltpu.SemaphoreType.DMA((MOE_TMAX // MOE_R,)),
            pltpu.SemaphoreType.DMA(()),
            pltpu.SemaphoreType.DMA((2, 2)),
        ],
    )
    return pl.pallas_call(
        _moe_kernel,
        grid_spec=grid_spec,
        out_shape=jax.ShapeDtypeStruct((n_rows, d), F32),
        compiler_params=_cparams(("arbitrary",)),
        name="moe_experts",
    )(pad1(item_e), pad1(item_row0), pad1(item_nsub), pad1(item_nvalid),
      xs, w_gate_up, b1, w_down, b2)


def _combine_kernel(dest_ref, dest_next_ref, y_hbm, gw_ref, x1_ref, mod_ref, g_ref, o_ref,
                    buf, sem):
    i = pl.program_id(0)
    n = COMB_TM * TOP_K
    slot = i & 1

    blocks_per_k = COMB_TM // DMA_UNROLL

    def gather(d_ref, sl):
        def block(o, c):
            k = o if blocks_per_k == 1 else lax.div(o, blocks_per_k)
            tok0 = pl.multiple_of((o & (blocks_per_k - 1)) * DMA_UNROLL, DMA_UNROLL)
            for j in range(DMA_UNROLL):
                pltpu.make_async_copy(
                    y_hbm.at[pl.ds(d_ref[0, 0, o * DMA_UNROLL + j], 1)],
                    buf.at[sl, k, pl.ds(tok0 + j, 1)],
                    sem.at[sl]).start(priority=j % DMA_PRIORITIES)
            return c

        lax.fori_loop(0, n // DMA_UNROLL, block, 0)

    @pl.when(i == 0)
    def _():
        gather(dest_ref, 0)

    @pl.when(i + 1 < pl.num_programs(0))
    def _():
        gather(dest_next_ref, 1 - slot)

    pltpu.make_async_copy(buf.at[slot], buf.at[slot], sem.at[slot]).wait()

    gw = gw_ref[...]
    y = buf[slot, 0] * gw[:, 0:1]
    for k in range(1, TOP_K):
        y = y + buf[slot, k] * gw[:, k:k + 1]
    m = mod_ref[0]
    o_ref[...] = x1_ref[...] + m[5:6, :] * (_rms(y) * g_ref[...])


def _combine(dest3, yb, gw, x1, mod6, g_post, seq):
    t, d = x1.shape
    steps = t // COMB_TM
    tiles_per_seq = seq // COMB_TM
    return pl.pallas_call(
        _combine_kernel,
        grid=(steps,),
        in_specs=[
            pl.BlockSpec((1, 1, COMB_TM * TOP_K), lambda i: (i, 0, 0), memory_space=pltpu.SMEM),
            pl.BlockSpec((1, 1, COMB_TM * TOP_K), lambda i: (jnp.minimum(i + 1, steps - 1), 0, 0),
                         memory_space=pltpu.SMEM),
            pl.BlockSpec(memory_space=pl.ANY),
            pl.BlockSpec((COMB_TM, TOP_K), lambda i: (i, 0)),
            pl.BlockSpec((COMB_TM, d), lambda i: (i, 0)),
            pl.BlockSpec((1, N_MOD, d), lambda i: (i // tiles_per_seq, 0, 0)),
            pl.BlockSpec((1, d), lambda i: (0, 0)),
        ],
        out_specs=pl.BlockSpec((COMB_TM, d), lambda i: (i, 0)),
        out_shape=jax.ShapeDtypeStruct((t, d), F32),
        scratch_shapes=[pltpu.VMEM((2, TOP_K, COMB_TM, d), F32), pltpu.SemaphoreType.DMA((2,))],
        compiler_params=_cparams(("arbitrary",)),
        name="moe_combine",
    )(dest3, dest3, yb, gw, x1, mod6, g_post)


def _per_step_entries(dest_kt, tm):
    k, t = dest_kt.shape
    return dest_kt.reshape(k, t // tm, tm).transpose(1, 0, 2).reshape(t // tm, 1, k * tm)


def _routing_tables(top_idx, rank, counts, n_items):
    pcounts = (counts + MOE_R - 1) // MOE_R * MOE_R
    pend = jnp.cumsum(pcounts)
    pstart = pend - pcounts
    experts = jnp.arange(N_EXPERTS, dtype=jnp.int32)
    start_of = jnp.sum(jnp.where(top_idx[:, :, None] == experts, pstart, 0), axis=-1)
    dest = start_of + rank

    tiles = (pcounts + MOE_TMAX - 1) // MOE_TMAX
    tend = jnp.cumsum(tiles)
    tstart = tend - tiles
    it = jnp.arange(n_items, dtype=jnp.int32)
    ie = jnp.minimum(jnp.sum((tend[None, :] <= it[:, None]).astype(jnp.int32), axis=1),
                     N_EXPERTS - 1)
    valid = it < tend[-1]
    local = it - tstart[ie]
    row0 = pstart[ie] + local * MOE_TMAX
    nrows = jnp.clip(pcounts[ie] - local * MOE_TMAX, 0, MOE_TMAX)
    nsub = jnp.where(valid, nrows // MOE_R, 0).astype(jnp.int32)
    nvalid = jnp.where(valid, jnp.clip(counts[ie] - local * MOE_TMAX, 0, MOE_TMAX), 0)
    row0 = jnp.where(valid, row0, 0).astype(jnp.int32)
    return dest.astype(jnp.int32), ie, row0, nsub, nvalid.astype(jnp.int32)


def kernel(x, c, w_mod, b_mod, g_pre_mix, g_post_mix, w_in, b_in, attn_sinks, w_pool_mix,
           pool_scale, w_attn_branch, w_pool_branch, w_out, g_pre_ffn, g_post_ffn, w_router,
           b_router, w_gate_up, b_gate_up, w_down, b_down):
    b, s, d = x.shape
    t = b * s
    depth = w_mod.shape[0]
    n_assign = t * TOP_K
    n_items = N_EXPERTS + n_assign // MOE_TMAX
    n_rows = n_assign + N_EXPERTS * MOE_R

    x2 = x.reshape(t, d)
    c_pad = jnp.pad(c, ((0, SUBLANES - b), (0, 0)))
    for l in range(depth):
        mod = _modulation(c_pad, w_mod[l], b_mod[l][None, :])
        mod6 = mod[:b].reshape(b, N_MOD, d)

        q, kv, u, gl = _in_projection(x2, mod6, g_pre_mix[l][None, :], w_in[l].astype(BF16),
                                      b_in[l][None, :], s)
        a = _attention(q, kv, attn_sinks[l], s)
        x1, h2, top_idx, gate_w, rank, counts = _mix(
            x2, a, u, gl, mod6, w_pool_mix[l].astype(BF16), pool_scale[l][None, :],
            w_attn_branch[l].astype(BF16), w_pool_branch[l].astype(BF16),
            w_out[l].astype(BF16), g_post_mix[l][None, :], g_pre_ffn[l][None, :],
            _split_bf16_rows(w_router[l].T), b_router[l][:, None], s)

        dest, item_e, item_row0, item_nsub, item_nvalid = _routing_tables(
            top_idx, rank, counts[:, 0].astype(jnp.int32), n_items)
        xs = _dispatch(dest.T.reshape(t // DISP_TM, 1, DISP_TM * TOP_K), h2, n_rows)
        yb = _moe(item_e, item_row0, item_nsub, item_nvalid, xs, w_gate_up[l], b_gate_up[l],
                  w_down[l], b_down[l])
        x2 = _combine(_per_step_entries(dest, COMB_TM), yb, gate_w.T, x1, mod6,
                      g_post_ffn[l][None, :], s)
    return x2.reshape(b, s, d)
```

```python
import functools

import jax
import jax.numpy as jnp
from jax import lax
from jax.experimental import pallas as pl
from jax.experimental.pallas import tpu as pltpu

F32 = jnp.float32
BF16 = jnp.bfloat16

D_MODEL = 2048
HEAD_DIM = 64
N_Q_HEADS = 16
N_KV_HEADS = 4
GROUP = N_Q_HEADS // N_KV_HEADS
WINDOW = 128
ATTN_SCALE = HEAD_DIM ** -0.5
assert ATTN_SCALE == 0.125
Q_W = N_Q_HEADS * HEAD_DIM
KV_W = N_KV_HEADS * HEAD_DIM
POOL_WINDOWS = (2, 4, 8, 16)
POOL_W = D_MODEL // 2
POOL_GC = POOL_W // len(POOL_WINDOWS)
POOL_HALO = 16
assert all(w & (w - 1) == 0 and 2 <= w <= POOL_HALO for w in POOL_WINDOWS)
N_EXPERTS = 32
TOP_K = 4
D_FF = D_MODEL
SWIGLU_LIMIT = 7.0
SWIGLU_ALPHA = 1.702
N_MOD = 6
RMS_EPS = 1e-6
NEG_INF = -1e30

LANES = 128
SUBLANES = 8
V7X_VMEM_BYTES = 64 * 1024 * 1024
VMEM_LIMIT_BYTES = V7X_VMEM_BYTES // 8 * 7

MOD_TN = 1024
PROJ_TM = 1024
PROJ_TN = 512
ATT_TQ = 512
MIX_TM = 256
MOE_R = 256
MOE_TMAX = 1536
MOE_FC = 256
DISP_TM = 1024
COMB_TM = 128
DMA_UNROLL = 128
DMA_PRIORITIES = 2

assert MOE_TMAX % MOE_R == 0 and D_FF % MOE_FC == 0 and MOE_FC % LANES == 0
assert DISP_TM * TOP_K % DMA_UNROLL == 0 and DMA_UNROLL % TOP_K == 0
assert COMB_TM % DMA_UNROLL == 0 and (COMB_TM // DMA_UNROLL) & (COMB_TM // DMA_UNROLL - 1) == 0


def _cparams(sem):
    return pltpu.CompilerParams(dimension_semantics=sem, vmem_limit_bytes=VMEM_LIMIT_BYTES)


def _rms(x):
    return x * lax.rsqrt(jnp.mean(x * x, axis=-1, keepdims=True) + RMS_EPS)


def _mod_kernel(c_ref, w_ref, b_ref, o_ref):
    c = c_ref[...]
    s = c * jax.nn.sigmoid(c)
    o_ref[...] = jnp.dot(s.astype(BF16), w_ref[...].astype(BF16),
                         preferred_element_type=F32) + b_ref[...]


def _modulation(c_pad, w_mod, b_mod):
    rows, d = c_pad.shape
    n = w_mod.shape[1]
    return pl.pallas_call(
        _mod_kernel,
        grid=(n // MOD_TN,),
        in_specs=[
            pl.BlockSpec((rows, d), lambda j: (0, 0)),
            pl.BlockSpec((d, MOD_TN), lambda j: (0, j)),
            pl.BlockSpec((1, MOD_TN), lambda j: (0, j)),
        ],
        out_specs=pl.BlockSpec((rows, MOD_TN), lambda j: (0, j)),
        out_shape=jax.ShapeDtypeStruct((rows, n), F32),
        compiler_params=_cparams(("arbitrary",)),
        name="modulation",
    )(c_pad, w_mod, b_mod)


_Q_TILES = Q_W // PROJ_TN
_KV_TILES = (2 * KV_W) // PROJ_TN
_U_TILES = POOL_W // PROJ_TN
_G_TILES = (2 * D_MODEL) // PROJ_TN
_KV_T0 = _Q_TILES
_U_T0 = _KV_T0 + _KV_TILES
_G_T0 = _U_T0 + _U_TILES
_PROJ_TILES = _G_T0 + _G_TILES


def _proj_kernel(x_ref, mod_ref, g_ref, w_ref, b_ref, q_ref, kv_ref, u_ref, gl_ref, h_ref):
    j = pl.program_id(1)

    @pl.when(j == 0)
    def _():
        m = mod_ref[0]
        h = _rms(x_ref[...]) * g_ref[...] * (1.0 + m[1:2, :]) + m[0:1, :]
        h_ref[...] = h.astype(BF16)

    def project(o_ref, scale=None):
        y = jnp.dot(h_ref[...], w_ref[...], preferred_element_type=F32) + b_ref[...]
        if scale is not None:
            y = y * scale
        o_ref[...] = y.astype(o_ref.dtype)

    @pl.when(j < _KV_T0)
    def _():
        project(q_ref, ATTN_SCALE)

    @pl.when((j >= _KV_T0) & (j < _U_T0))
    def _():
        project(kv_ref)

    @pl.when((j >= _U_T0) & (j < _G_T0))
    def _():
        project(u_ref)

    @pl.when(j >= _G_T0)
    def _():
        project(gl_ref)


def _in_projection(x2, mod6, g_pre, w_in, b_in, seq):
    t, d = x2.shape
    tiles_per_seq = seq // PROJ_TM

    def clampj(j, lo, n):
        return jnp.clip(j - lo, 0, n - 1)

    return pl.pallas_call(
        _proj_kernel,
        grid=(t // PROJ_TM, _PROJ_TILES),
        in_specs=[
            pl.BlockSpec((PROJ_TM, d), lambda i, j: (i, 0)),
            pl.BlockSpec((1, N_MOD, d), lambda i, j: (i // tiles_per_seq, 0, 0)),
            pl.BlockSpec((1, d), lambda i, j: (0, 0)),
            pl.BlockSpec((d, PROJ_TN), lambda i, j: (0, j)),
            pl.BlockSpec((1, PROJ_TN), lambda i, j: (0, j)),
        ],
        out_specs=[
            pl.BlockSpec((PROJ_TM, PROJ_TN), lambda i, j: (i, clampj(j, 0, _Q_TILES))),
            pl.BlockSpec((PROJ_TM, PROJ_TN), lambda i, j: (i, clampj(j, _KV_T0, _KV_TILES))),
            pl.BlockSpec((PROJ_TM, PROJ_TN), lambda i, j: (i, clampj(j, _U_T0, _U_TILES))),
            pl.BlockSpec((PROJ_TM, PROJ_TN), lambda i, j: (i, clampj(j, _G_T0, _G_TILES))),
        ],
        out_shape=[
            jax.ShapeDtypeStruct((t, Q_W), BF16),
            jax.ShapeDtypeStruct((t, 2 * KV_W), BF16),
            jax.ShapeDtypeStruct((t, POOL_W), F32),
            jax.ShapeDtypeStruct((t, 2 * D_MODEL), F32),
        ],
        scratch_shapes=[pltpu.VMEM((PROJ_TM, d), BF16)],
        compiler_params=_cparams(("arbitrary", "arbitrary")),
        name="in_projection",
    )(x2, mod6, g_pre, w_in, b_in)


def _attn_kernel(sink_ref, q_ref, kvp_ref, kvc_ref, o_ref, *, blocks_per_seq):
    i = pl.program_id(0)
    nq = ATT_TQ // WINDOW
    rows = N_Q_HEADS * WINDOW
    hrows = GROUP * WINDOW
    r_iota = lax.broadcasted_iota(jnp.int32, (rows, 2 * WINDOW), 0)
    k_iota = lax.broadcasted_iota(jnp.int32, (rows, 2 * WINDOW), 1)
    qi = r_iota % WINDOW
    band = (k_iota > qi) & (k_iota <= qi + WINDOW)
    head_of_row = lax.broadcasted_iota(jnp.int32, (rows, 1), 0) // WINDOW
    sink = jnp.zeros((rows, 1), F32)
    for qh in range(N_Q_HEADS):
        sink = jnp.where(head_of_row == qh, sink_ref[qh], sink)

    for n in range(nq):
        if n == 0:
            first = (i % blocks_per_seq) == 0
            mask = band & ((k_iota >= WINDOW) | jnp.logical_not(first))
            kv_prev = kvp_ref[...]
        else:
            mask = band
            kv_prev = kvc_ref[(n - 1) * WINDOW:n * WINDOW, :]
        kv_cur = kvc_ref[n * WINDOW:(n + 1) * WINDOW, :]
        kv = jnp.concatenate([kv_prev, kv_cur], axis=0)
        qn = q_ref[n * WINDOW:(n + 1) * WINDOW, :]
        s = jnp.concatenate(
            [lax.dot_general(
                jnp.concatenate(
                    [qn[:, (h * GROUP + g) * HEAD_DIM:(h * GROUP + g + 1) * HEAD_DIM]
                     for g in range(GROUP)], axis=0),
                kv[:, h * HEAD_DIM:(h + 1) * HEAD_DIM],
                (((1,), (1,)), ((), ())), preferred_element_type=F32)
             for h in range(N_KV_HEADS)], axis=0)
        s = jnp.where(mask, s, NEG_INF)
        m = jnp.maximum(jnp.max(s, axis=-1, keepdims=True), sink)
        p = jnp.exp(s - m)
        inv = 1.0 / (jnp.sum(p, axis=-1, keepdims=True) + jnp.exp(sink - m))
        pb = p.astype(BF16)
        for h in range(N_KV_HEADS):
            v = kv[:, KV_W + h * HEAD_DIM:KV_W + (h + 1) * HEAD_DIM]
            o = jnp.dot(pb[h * hrows:(h + 1) * hrows, :], v, preferred_element_type=F32)
            o = o * inv[h * hrows:(h + 1) * hrows, :]
            for g in range(GROUP):
                c0 = (h * GROUP + g) * HEAD_DIM
                o_ref[n * WINDOW:(n + 1) * WINDOW, c0:c0 + HEAD_DIM] = (
                    o[g * WINDOW:(g + 1) * WINDOW, :].astype(BF16))


def _attention(q, kv, sinks, seq):
    t = q.shape[0]
    blocks_per_seq = seq // ATT_TQ
    wpb = ATT_TQ // WINDOW
    return pl.pallas_call(
        functools.partial(_attn_kernel, blocks_per_seq=blocks_per_seq),
        grid=(t // ATT_TQ,),
        in_specs=[
            pl.BlockSpec(memory_space=pltpu.SMEM),
            pl.BlockSpec((ATT_TQ, Q_W), lambda i: (i, 0)),
            pl.BlockSpec((WINDOW, 2 * KV_W), lambda i: (jnp.maximum(i * wpb - 1, 0), 0)),
            pl.BlockSpec((ATT_TQ, 2 * KV_W), lambda i: (i, 0)),
        ],
        out_specs=pl.BlockSpec((ATT_TQ, Q_W), lambda i: (i, 0)),
        out_shape=jax.ShapeDtypeStruct((t, Q_W), BF16),
        compiler_params=_cparams(("arbitrary",)),
        name="swa_attention",
    )(sinks, q, kv, kv)


def _mix_kernel(x_ref, a_ref, up_ref, u_ref, gl_ref, mod_ref, wpm_ref, ps_ref, wa_ref, wp_ref,
                wo_ref, gpost_ref, gffn_ref, wrt_ref, brt_ref,
                x1_ref, h2_ref, idx_ref, gw_ref, rank_ref, cnt_ref, ubuf_ref, pool_tmp, base_ref,
                *, tiles_per_seq):
    i = pl.program_id(0)
    tm = MIX_TM
    ne = N_EXPERTS
    seq_tile = i % tiles_per_seq
    m = mod_ref[0]

    pad = SUBLANES
    top = pad + POOL_HALO
    halo = jnp.where(seq_tile == 0, 0.0, up_ref[...])
    ubuf_ref[0:pad, :] = jnp.zeros((pad, POOL_W), F32)
    ubuf_ref[pad:top, :] = halo
    ubuf_ref[top:, :] = u_ref[...]
    pool_tmp[:, 0:pad, :] = jnp.zeros((2, pad, POOL_GC), F32)
    ya = jnp.dot(a_ref[...], wa_ref[...], preferred_element_type=F32)
    pos = seq_tile * tm + lax.broadcasted_iota(jnp.int32, (tm, 1), 0)
    p_parts = []
    for g, w in enumerate(POOL_WINDOWS):
        c0 = g * POOL_GC
        ug = ubuf_ref[top:top + tm, c0:c0 + POOL_GC]
        n_stage = w.bit_length() - 1
        read = lambda lo, hi: ubuf_ref[lo:hi, c0:c0 + POOL_GC]
        for j in range(1, n_stage + 1):
            d = 1 << (j - 1)
            if j < n_stage:
                pool_tmp[j % 2, pad:top + tm, :] = read(pad, top + tm) + read(pad - d, top + tm - d)
                read = functools.partial(lambda lo, hi, sl: pool_tmp[sl, lo:hi, :], sl=j % 2)
            else:
                acc = read(top, top + tm) + read(top - d, top + tm - d)
        cnt = jnp.minimum(pos + 1, w).astype(F32)
        pg = acc / cnt - ug
        p_parts.append(jnp.dot(pg.astype(BF16), wpm_ref[g], preferred_element_type=F32))
    p = jnp.concatenate(p_parts, axis=-1) * ps_ref[...]

    yp =jnp.dot(p.astype(BF16), wp_ref[...], preferred_element_type=F32)
    gates_a = jax.nn.sigmoid(gl_ref[:, 0:D_MODEL])
    gates_p = jax.nn.sigmoid(gl_ref[:, D_MODEL:2 * D_MODEL])
    merged = gates_a * ya + gates_p * yp
    mix = jnp.dot(merged.astype(BF16), wo_ref[...], preferred_element_type=F32)
    x1 = x_ref[...] + m[2:3, :] * (_rms(mix) * gpost_ref[...])
    x1_ref[...] = x1

    h2 = _rms(x1) * gffn_ref[...] * (1.0 + m[4:5, :]) + m[3:4, :]
    h2_ref[...] = h2

    h_hi = h2.astype(BF16)
    h_lo = (h2 - h_hi.astype(F32)).astype(BF16)
    prod = lax.dot_general(wrt_ref[...], jnp.concatenate([h_hi, h_lo], axis=0),
                           (((1,), (1,)), ((), ())), preferred_element_type=F32)
    lg = ((prod[0:ne, 0:tm] + prod[ne:, 0:tm]) + (prod[0:ne, tm:] + prod[ne:, tm:])) + brt_ref[...]
    e_iota = lax.broadcasted_iota(jnp.int32, (ne, tm), 0)
    vals, idxs, picks = [], [], []
    for _ in range(TOP_K):
        mx = jnp.max(lg, axis=0, keepdims=True)
        ix = jnp.min(jnp.where(lg == mx, e_iota, ne), axis=0, keepdims=True)
        pick = e_iota == ix
        vals.append(mx)
        idxs.append(ix)
        picks.append(pick)
        lg = jnp.where(pick, -jnp.inf, lg)
    ex = [jnp.exp(v - vals[0]) for v in vals]
    den = ex[0] + ex[1] + ex[2] + ex[3]
    gw_ref[...] = jnp.concatenate([e / den for e in ex], axis=0)
    idx_ref[...] = jnp.concatenate(idxs, axis=0)

    @pl.when(i == 0)
    def _():
        base_ref[...] = jnp.zeros_like(base_ref)

    sel = (picks[0] | picks[1] | picks[2] | picks[3]).astype(F32)
    earlier = (lax.broadcasted_iota(jnp.int32, (tm, tm), 0)
               < lax.broadcasted_iota(jnp.int32, (tm, tm), 1)).astype(BF16)
    rank = jnp.dot(sel.astype(BF16), earlier, preferred_element_type=F32) + base_ref[...]
    rank_ref[...] = jnp.concatenate(
        [jnp.sum(jnp.where(pk, rank, 0.0), axis=0, keepdims=True) for pk in picks],
        axis=0).astype(jnp.int32)
    total = base_ref[...] + jnp.sum(sel, axis=1, keepdims=True)
    base_ref[...] = total
    cnt_ref[...] = total


def _split_bf16_rows(w):
    hi = w.astype(BF16)
    lo = (w - hi.astype(F32)).astype(BF16)
    return jnp.concatenate([hi, lo], axis=0)


def _mix(x2, a, u, gl, mod6, wpm, ps, wa, wp, wo, gpost, gffn, wrt, brt, seq):
    t, d = x2.shape
    tm = MIX_TM
    tiles_per_seq = seq // tm
    hb = tm // POOL_HALO
    const2 = lambda i: (0, 0)
    return pl.pallas_call(
        functools.partial(_mix_kernel, tiles_per_seq=tiles_per_seq),
        grid=(t // tm,),
        in_specs=[
            pl.BlockSpec((tm, d), lambda i: (i, 0)),
            pl.BlockSpec((tm, Q_W), lambda i: (i, 0)),
            pl.BlockSpec((POOL_HALO, POOL_W), lambda i: (jnp.maximum(i * hb - 1, 0), 0)),
            pl.BlockSpec((tm, POOL_W), lambda i: (i, 0)),
            pl.BlockSpec((tm, 2 * d), lambda i: (i, 0)),
            pl.BlockSpec((1, N_MOD, d), lambda i: (i // tiles_per_seq, 0, 0)),
            pl.BlockSpec(wpm.shape, lambda i: (0, 0, 0), pipeline_mode=pl.Buffered(1)),
            pl.BlockSpec((1, POOL_W), const2),
            pl.BlockSpec(wa.shape, const2, pipeline_mode=pl.Buffered(1)),
            pl.BlockSpec(wp.shape, const2, pipeline_mode=pl.Buffered(1)),
            pl.BlockSpec(wo.shape, const2, pipeline_mode=pl.Buffered(1)),
            pl.BlockSpec((1, d), const2),
            pl.BlockSpec((1, d), const2),
            pl.BlockSpec(wrt.shape, const2),
            pl.BlockSpec((N_EXPERTS, 1), const2),
        ],
        out_specs=[
            pl.BlockSpec((tm, d), lambda i: (i, 0)),
            pl.BlockSpec((tm, d), lambda i: (i, 0)),
            pl.BlockSpec((TOP_K, tm), lambda i: (0, i)),
            pl.BlockSpec((TOP_K, tm), lambda i: (0, i)),
            pl.BlockSpec((TOP_K, tm), lambda i: (0, i)),
            pl.BlockSpec((N_EXPERTS, 1), const2),
        ],
        out_shape=[
            jax.ShapeDtypeStruct((t, d), F32),
            jax.ShapeDtypeStruct((t, d), F32),
            jax.ShapeDtypeStruct((TOP_K, t), jnp.int32),
            jax.ShapeDtypeStruct((TOP_K, t), F32),
            jax.ShapeDtypeStruct((TOP_K, t), jnp.int32),
            jax.ShapeDtypeStruct((N_EXPERTS, 1), F32),
        ],
        scratch_shapes=[pltpu.VMEM((SUBLANES + POOL_HALO + tm, POOL_W), F32),
                        pltpu.VMEM((2, SUBLANES + POOL_HALO + tm, POOL_GC), F32),
                        pltpu.VMEM((N_EXPERTS, 1), F32)],
        compiler_params=_cparams(("arbitrary",)),
        name="mix_merge_router",
    )(x2, a, u, u, gl, mod6, wpm, ps, wa, wp, wo, gpost, gffn, wrt, brt)


def _dispatch_kernel(dest_ref, h_ref, xs_hbm, sem):
    n = DISP_TM * TOP_K

    tok_per_block = DMA_UNROLL // TOP_K

    def block(o, c):
        tok0 = pl.multiple_of(o * tok_per_block, tok_per_block)
        for j in range(DMA_UNROLL):
            pltpu.make_async_copy(
                h_ref.at[pl.ds(tok0 + j // TOP_K, 1)],
                xs_hbm.at[pl.ds(dest_ref[0, 0, o * DMA_UNROLL + j], 1)],
                sem).start(priority=j % DMA_PRIORITIES)
        return c

    lax.fori_loop(0, n // DMA_UNROLL, block, 0)
    pltpu.make_async_copy(xs_hbm.at[pl.ds(0, n)], xs_hbm.at[pl.ds(0, n)], sem).wait()


def _dispatch(dest3, h2, n_rows):
    t, d = h2.shape
    steps = t // DISP_TM
    return pl.pallas_call(
        _dispatch_kernel,
        grid=(steps,),
        in_specs=[
            pl.BlockSpec((1, 1, DISP_TM * TOP_K), lambda i: (i, 0, 0), memory_space=pltpu.SMEM),
            pl.BlockSpec((DISP_TM, d), lambda i: (i, 0)),
        ],
        out_specs=pl.BlockSpec(memory_space=pl.ANY),
        out_shape=jax.ShapeDtypeStruct((n_rows, d), F32),
        scratch_shapes=[pltpu.SemaphoreType.DMA(())],
        compiler_params=_cparams(("arbitrary",)),
        name="moe_dispatch",
    )(dest3, h2)


_MOE_NC = D_FF // MOE_FC
_GU_W = 2 * MOE_FC
assert _MOE_NC >= 4 and _MOE_NC % 2 == 0


def _moe_kernel(ie_ref, ir_ref, ins_ref, inv_ref, xs_hbm, w1_hbm, b1_ref, w2_hbm, b2_ref, y_hbm,
                xs_v, acc_v, stage_v, w1f_v, w2f_v, w1_v, w2i_v, w2_v, gu_v,
                in_sem, out_sem, w_sem):
    i = pl.program_id(0)

    @pl.when(ins_ref[i] > 0)
    def _():
        _moe_item(i, ie_ref, ir_ref, ins_ref, inv_ref, xs_hbm, w1_hbm, b1_ref, w2_hbm, b2_ref,
                  y_hbm, xs_v, acc_v, stage_v, w1f_v, w2f_v, w1_v, w2i_v, w2_v, gu_v,
                  in_sem, out_sem, w_sem)


def _moe_item(i, ie_ref, ir_ref, ins_ref, inv_ref, xs_hbm, w1_hbm, b1_ref, w2_hbm, b2_ref, y_hbm,
              xs_v, acc_v, stage_v, w1f_v, w2f_v, w1_v, w2i_v, w2_v, gu_v,
              in_sem, out_sem, w_sem):
    row0 = ir_ref[i]
    nsub = ins_ref[i]
    nvalid = inv_ref[i]
    has_next = ins_ref[i + 1] > 0
    r = MOE_R
    last_c = _MOE_NC - 1

    def rows(s):
        return pl.ds(pl.multiple_of(s * r, r), r)

    def in_copy(item, s):
        src0 = ir_ref[item] + s * r
        return pltpu.make_async_copy(
            xs_hbm.at[pl.ds(pl.multiple_of(src0, r), r)], stage_v.at[rows(s)], in_sem.at[s])

    def out_copy(s):
        return pltpu.make_async_copy(
            acc_v.at[rows(s)], y_hbm.at[pl.ds(pl.multiple_of(row0 + s * r, r), r)], out_sem)

    def weight_copies(item, c, slot):
        e = ie_ref[item]
        return (
            pltpu.make_async_copy(
                w1_hbm.at[e, :, pl.ds(pl.multiple_of(c * _GU_W, _GU_W), _GU_W)],
                w1f_v.at[slot], w_sem.at[0, slot]),
            pltpu.make_async_copy(
                w2_hbm.at[e, pl.ds(pl.multiple_of(c * MOE_FC, MOE_FC), MOE_FC), :],
                w2f_v.at[slot], w_sem.at[1, slot]),
        )

    def start_weights(item, c, slot):
        for cp in weight_copies(item, c, slot):
            cp.start()

    def wait_weights(c, slot):
        for cp in weight_copies(i, c, slot):
            cp.wait()

    def each(n, fn):
        def body(s, carry):
            fn(s)
            return carry
        lax.fori_loop(0, n, body, 0)

    def prep_w1(slot):
        w1_v[...] = w1f_v[slot].astype(BF16)

    def prep_w2(slot):
        half = LANES // 2
        for cb in range(D_MODEL // LANES):
            cs = slice(cb * LANES, (cb + 1) * LANES)
            for blk in range(MOE_FC // LANES):
                b0 = blk * LANES
                w2i_v[cb, pl.ds(b0, half, stride=2), :] = w2f_v[slot, b0:b0 + half, cs]
                w2i_v[cb, pl.ds(b0 + 1, half, stride=2), :] = w2f_v[slot, b0 + half:b0 + LANES, cs]
            w2_v[slot, :, cs] = w2i_v[cb].astype(BF16)

    lane = lax.broadcasted_iota(jnp.int32, (r, LANES), 1)
    even = (lane % 2) == 0

    def gate_up(s, c):
        b1 = b1_ref[0, :, pl.ds(pl.multiple_of(c * _GU_W, _GU_W), _GU_W)]
        return jnp.dot(xs_v[rows(s), :], w1_v[...], preferred_element_type=F32) + b1

    def activation(gu):
        glu = jnp.minimum(gu, SWIGLU_LIMIT)
        f = glu * jax.nn.sigmoid(SWIGLU_ALPHA * glu)
        lin = jnp.clip(gu, -SWIGLU_LIMIT, SWIGLU_LIMIT) + 1.0
        parts = []
        for blk in range(MOE_FC // LANES):
            ca = 2 * blk * LANES
            cb = ca + LANES
            za = f[:, ca:ca + LANES] * pltpu.roll(lin[:, ca:ca + LANES], LANES - 1, 1)
            zb = pltpu.roll(f[:, cb:cb + LANES], 1, 1) * lin[:, cb:cb + LANES]
            parts.append(jnp.where(even, za, zb))
        return jnp.concatenate(parts, axis=-1).astype(BF16)

    def load_rows(s):
        in_copy(i, s).wait()
        live = (s * r + lax.broadcasted_iota(jnp.int32, (r, 1), 0)) < nvalid
        xs_v[rows(s), :] = jnp.where(live, stage_v[rows(s), :], 0.0).astype(BF16)

    def down(s, gu, slot):
        return jnp.dot(activation(gu), w2_v[slot], preferred_element_type=F32)

    def pipelined_rows(c, slot, finish, first, tslot, after_first):
        g0 = gate_up(0, c)
        gu_v[0] = g0
        gu_v[tslot] = g0
        after_first()

        def step(s, g):
            if first:
                load_rows(s + 1)
            nxt = gate_up(s + 1, c)
            finish(s, gu_v[g], slot)
            gu_v[1 - g] = nxt
            gu_v[tslot] = nxt

        def pair(p, carry):
            step(2 * p, 0)
            step(2 * p + 1, 1)
            return carry

        def single(_, carry):
            step(nsub - 2, 0)
            return carry

        n_steps = nsub - 1
        lax.fori_loop(0, lax.shift_right_logical(n_steps, 1), pair, 0)
        lax.fori_loop(0, n_steps & 1, single, 0)

    def finish_first(s, gu, slot):
        acc_v[rows(s), :] = down(s, gu, slot)

    def finish_mid(s, gu, slot):
        acc_v[rows(s), :] = acc_v[rows(s), :] + down(s, gu, slot)

    def finish_last(s, gu, slot):
        acc_v[rows(s), :] = acc_v[rows(s), :] + down(s, gu, slot) + b2_ref[0]
        out_copy(s).start()

    tail = nsub - 1

    @pl.when(i == 0)
    def _():
        start_weights(i, 0, 0)
        each(nsub, lambda s: in_copy(i, s).start())

    wait_weights(0, 0)
    start_weights(i, 1, 1)

    @pl.when(i > 0)
    def _():
        each(ins_ref[jnp.maximum(i - 1, 0)], lambda s: out_copy(s).wait())

    prep_w1(0)
    load_rows(0)
    pipelined_rows(0, 0, finish_first, True, 2, lambda: prep_w2(0))

    @pl.when(has_next)
    def _():
        each(ins_ref[i + 1], lambda s: in_copy(i + 1, s).start())

    def mid_chunk(c, slot):
        wait_weights(c, slot)
        start_weights(i, c + 1, 1 - slot)
        prep_w1(slot)

        def after_first():
            prev = down(tail, gu_v[3 - slot], 1 - slot)
            acc_v[rows(tail), :] = jnp.where(c == 1, prev, acc_v[rows(tail), :] + prev)
            prep_w2(slot)

        pipelined_rows(c, slot, finish_mid, False, 2 + slot, after_first)

    def chunk_pair(p, carry):
        mid_chunk(2 * p + 1, 1)
        mid_chunk(2 * p + 2, 0)
        return carry

    lax.fori_loop(0, (last_c - 1) // 2, chunk_pair, 0)

    lslot = last_c & 1
    wait_weights(last_c, lslot)

    @pl.when(has_next)
    def _():
        start_weights(i + 1, 0, 1 - lslot)

    prep_w1(lslot)

    def last_after_first():
        finish_mid(tail, gu_v[3 - lslot], 1 - lslot)
        prep_w2(lslot)

    pipelined_rows(last_c, lslot, finish_last, False, 2 + lslot, last_after_first)
    finish_last(tail, gu_v[2 + lslot], lslot)

    @pl.when(jnp.logical_not(has_next))
    def _():
        each(nsub, lambda s: out_copy(s).wait())


def _moe(item_e, item_row0, item_nsub, item_nvalid, xs, w_gate_up, b_gate_up, w_down, b_down):
    n_items = item_e.shape[0]
    n_rows, d = xs.shape
    e = w_gate_up.shape[0]
    b1 = b_gate_up.reshape(e, 1, 2 * D_FF)
    b2 = b_down.reshape(e, 1, d)
    pad1 = lambda a: jnp.pad(a, (0, 1))

    grid_spec = pltpu.PrefetchScalarGridSpec(
        num_scalar_prefetch=4,
        grid=(n_items,),
        in_specs=[
            pl.BlockSpec(memory_space=pl.ANY),
            pl.BlockSpec(memory_space=pl.ANY),
            pl.BlockSpec((1, 1, 2 * D_FF), lambda i, ie, ir, ins, inv: (ie[i], 0, 0)),
            pl.BlockSpec(memory_space=pl.ANY),
            pl.BlockSpec((1, 1, d), lambda i, ie, ir, ins, inv: (ie[i], 0, 0)),
        ],
        out_specs=pl.BlockSpec(memory_space=pl.ANY),
        scratch_shapes=[
            pltpu.VMEM((MOE_TMAX, d), BF16),
            pltpu.VMEM((MOE_TMAX, d), F32),
            pltpu.VMEM((MOE_TMAX, d), F32),
            pltpu.VMEM((2, d, _GU_W), F32),
            pltpu.VMEM((2, MOE_FC, d), F32),
            pltpu.VMEM((d, _GU_W), BF16),
            pltpu.VMEM((d // LANES, MOE_FC, LANES), F32),
            pltpu.VMEM((2, MOE_FC, d), BF16),
            pltpu.VMEM((4, MOE_R, _GU_W), F32),
            pltpu.SemaphoreType.DMA((MOE_TMAX // MOE_R,)),
            pltpu.SemaphoreType.DMA(()),
            pltpu.SemaphoreType.DMA((2, 2)),
        ],
    )
    return pl.pallas_call(
        _moe_kernel,
        grid_spec=grid_spec,
        out_shape=jax.ShapeDtypeStruct((n_rows, d), F32),
        compiler_params=_cparams(("arbitrary",)),
        name="moe_experts",
    )(pad1(item_e), pad1(item_row0), pad1(item_nsub), pad1(item_nvalid),
      xs, w_gate_up, b1, w_down, b2)


def _combine_kernel(dest_ref, dest_next_ref, y_hbm, gw_ref, x1_ref, mod_ref, g_ref, o_ref,
                    buf, sem):
    i = pl.program_id(0)
    n = COMB_TM * TOP_K
    slot = i & 1

    blocks_per_k = COMB_TM // DMA_UNROLL

    def gather(d_ref, sl):
        def block(o, c):
            k = o if blocks_per_k == 1 else lax.div(o, blocks_per_k)
            tok0 = pl.multiple_of((o & (blocks_per_k - 1)) * DMA_UNROLL, DMA_UNROLL)
            for j in range(DMA_UNROLL):
                pltpu.make_async_copy(
                    y_hbm.at[pl.ds(d_ref[0, 0, o * DMA_UNROLL + j], 1)],
                    buf.at[sl, k, pl.ds(tok0 + j, 1)],
                    sem.at[sl]).start(priority=j % DMA_PRIORITIES)
            return c

        lax.fori_loop(0, n // DMA_UNROLL, block, 0)

    @pl.when(i == 0)
    def _():
        gather(dest_ref, 0)

    @pl.when(i + 1 < pl.num_programs(0))
    def _():
        gather(dest_next_ref, 1 - slot)

    pltpu.make_async_copy(buf.at[slot], buf.at[slot], sem.at[slot]).wait()

    gw = gw_ref[...]
    y = buf[slot, 0] * gw[:, 0:1]
    for k in range(1, TOP_K):
        y = y + buf[slot, k] * gw[:, k:k + 1]
    m = mod_ref[0]
    o_ref[...] = x1_ref[...] + m[5:6, :] * (_rms(y) * g_ref[...])


def _combine(dest3, yb, gw, x1, mod6, g_post, seq):
    t, d = x1.shape
    steps = t // COMB_TM
    tiles_per_seq = seq // COMB_TM
    return pl.pallas_call(
        _combine_kernel,
        grid=(steps,),
        in_specs=[
            pl.BlockSpec((1, 1, COMB_TM * TOP_K), lambda i: (i, 0, 0), memory_space=pltpu.SMEM),
            pl.BlockSpec((1, 1, COMB_TM * TOP_K), lambda i: (jnp.minimum(i + 1, steps - 1), 0, 0),
                         memory_space=pltpu.SMEM),
            pl.BlockSpec(memory_space=pl.ANY),
            pl.BlockSpec((COMB_TM, TOP_K), lambda i: (i, 0)),
            pl.BlockSpec((COMB_TM, d), lambda i: (i, 0)),
            pl.BlockSpec((1, N_MOD, d), lambda i: (i // tiles_per_seq, 0, 0)),
            pl.BlockSpec((1, d), lambda i: (0, 0)),
        ],
        out_specs=pl.BlockSpec((COMB_TM, d), lambda i: (i, 0)),
        out_shape=jax.ShapeDtypeStruct((t, d), F32),
        scratch_shapes=[pltpu.VMEM((2, TOP_K, COMB_TM, d), F32), pltpu.SemaphoreType.DMA((2,))],
        compiler_params=_cparams(("arbitrary",)),
        name="moe_combine",
    )(dest3, dest3, yb, gw, x1, mod6, g_post)


def _per_step_entries(dest_kt, tm):
    k, t = dest_kt.shape
    return dest_kt.reshape(k, t // tm, tm).transpose(1, 0, 2).reshape(t // tm, 1, k * tm)


def _routing_tables(top_idx, rank, counts, n_items):
    pcounts = (counts + MOE_R - 1) // MOE_R * MOE_R
    pend = jnp.cumsum(pcounts)
    pstart = pend - pcounts
    experts = jnp.arange(N_EXPERTS, dtype=jnp.int32)
    start_of = jnp.sum(jnp.where(top_idx[:, :, None] == experts, pstart, 0), axis=-1)
    dest = start_of + rank

    tiles = (pcounts + MOE_TMAX - 1) // MOE_TMAX
    tend = jnp.cumsum(tiles)
    tstart = tend - tiles
    it = jnp.arange(n_items, dtype=jnp.int32)
    ie = jnp.minimum(jnp.sum((tend[None, :] <= it[:, None]).astype(jnp.int32), axis=1),
                     N_EXPERTS - 1)
    valid = it < tend[-1]
    local = it - tstart[ie]
    row0 = pstart[ie] + local * MOE_TMAX
    nrows = jnp.clip(pcounts[ie] - local * MOE_TMAX, 0, MOE_TMAX)
    nsub = jnp.where(valid, nrows // MOE_R, 0).astype(jnp.int32)
    nvalid = jnp.where(valid, jnp.clip(counts[ie] - local * MOE_TMAX, 0, MOE_TMAX), 0)
    row0 = jnp.where(valid, row0, 0).astype(jnp.int32)
    return dest.astype(jnp.int32), ie, row0, nsub, nvalid.astype(jnp.int32)


def kernel(x, c, w_mod, b_mod, g_pre_mix, g_post_mix, w_in, b_in, attn_sinks, w_pool_mix,
           pool_scale, w_attn_branch, w_pool_branch, w_out, g_pre_ffn, g_post_ffn, w_router,
           b_router, w_gate_up, b_gate_up, w_down, b_down):
    b, s, d = x.shape
    t = b * s
    depth = w_mod.shape[0]
    n_assign = t * TOP_K
    n_items = N_EXPERTS + n_assign // MOE_TMAX
    n_rows = n_assign + N_EXPERTS * MOE_R

    x2 = x.reshape(t, d)
    c_pad = jnp.pad(c, ((0, SUBLANES - b), (0, 0)))
    for l in range(depth):
        mod = _modulation(c_pad, w_mod[l], b_mod[l][None, :])
        mod6 = mod[:b].reshape(b, N_MOD, d)

        q, kv, u, gl = _in_projection(x2, mod6, g_pre_mix[l][None, :], w_in[l].astype(BF16),
                                      b_in[l][None, :], s)
        a = _attention(q, kv, attn_sinks[l], s)
        x1, h2, top_idx, gate_w, rank, counts = _mix(
            x2, a, u, gl, mod6, w_pool_mix[l].astype(BF16), pool_scale[l][None, :],
            w_attn_branch[l].astype(BF16), w_pool_branch[l].astype(BF16),
            w_out[l].astype(BF16), g_post_mix[l][None, :], g_pre_ffn[l][None, :],
            _split_bf16_rows(w_router[l].T), b_router[l][:, None], s)

        dest, item_e, item_row0, item_nsub, item_nvalid = _routing_tables(
            top_idx, rank, counts[:, 0].astype(jnp.int32), n_items)
        xs = _dispatch(dest.T.reshape(t // DISP_TM, 1, DISP_TM * TOP_K), h2, n_rows)
        yb = _moe(item_e, item_row0, item_nsub, item_nvalid, xs, w_gate_up[l], b_gate_up[l],
                  w_down[l], b_down[l])
        x2 = _combine(_per_step_entries(dest, COMB_TM), yb, gate_w.T, x1, mod6,
                      g_post_ffn[l][None, :], s)
    return x2.reshape(b, s, d)
```

```python
import functools

import jax
import jax.numpy as jnp
from jax import lax
from jax.experimental import pallas as pl
from jax.experimental.pallas import tpu as pltpu

F32 = jnp.float32
BF16 = jnp.bfloat16

D_MODEL = 2048
HEAD_DIM = 64
N_Q_HEADS = 16
N_KV_HEADS = 4
GROUP = N_Q_HEADS // N_KV_HEADS
WINDOW = 128
ATTN_SCALE = HEAD_DIM ** -0.5
assert ATTN_SCALE == 0.125
Q_W = N_Q_HEADS * HEAD_DIM
KV_W = N_KV_HEADS * HEAD_DIM
POOL_WINDOWS = (2, 4, 8, 16)
POOL_W = D_MODEL // 2
POOL_GC = POOL_W // len(POOL_WINDOWS)
POOL_HALO = 16
assert all(w & (w - 1) == 0 and 2 <= w <= POOL_HALO for w in POOL_WINDOWS)
N_EXPERTS = 32
TOP_K = 4
D_FF = D_MODEL
SWIGLU_LIMIT = 7.0
SWIGLU_ALPHA = 1.702
N_MOD = 6
RMS_EPS = 1e-6
NEG_INF = -1e30

LANES = 128
SUBLANES = 8
V7X_VMEM_BYTES = 64 * 1024 * 1024
VMEM_LIMIT_BYTES = V7X_VMEM_BYTES // 8 * 7

MOD_TN = 1024
PROJ_TM = 2048
PROJ_XP = 512
PROJ_TN = 512
ATT_TQ = 512
MIX_TM = 256
MOE_R = 256
MOE_TMAX = 1536
MOE_FC = 256
DISP_TM = 1024
COMB_TM = 128
DMA_UNROLL = 128
DMA_PRIORITIES = 2

assert MOE_TMAX % MOE_R == 0 and D_FF % MOE_FC == 0 and MOE_FC % LANES == 0
assert DISP_TM * TOP_K % DMA_UNROLL == 0 and DMA_UNROLL % TOP_K == 0
assert COMB_TM % DMA_UNROLL == 0 and (COMB_TM // DMA_UNROLL) & (COMB_TM // DMA_UNROLL - 1) == 0


def _cparams(sem):
    return pltpu.CompilerParams(dimension_semantics=sem, vmem_limit_bytes=VMEM_LIMIT_BYTES)


def _rms(x):
    return x * lax.rsqrt(jnp.mean(x * x, axis=-1, keepdims=True) + RMS_EPS)


def _mod_kernel(c_ref, w_ref, b_ref, o_ref):
    c = c_ref[...]
    s = c * jax.nn.sigmoid(c)
    o_ref[...] = jnp.dot(s.astype(BF16), w_ref[...].astype(BF16),
                         preferred_element_type=F32) + b_ref[...]


def _modulation(c_pad, w_mod, b_mod):
    rows, d = c_pad.shape
    n = w_mod.shape[1]
    return pl.pallas_call(
        _mod_kernel,
        grid=(n // MOD_TN,),
        in_specs=[
            pl.BlockSpec((rows, d), lambda j: (0, 0)),
            pl.BlockSpec((d, MOD_TN), lambda j: (0, j)),
            pl.BlockSpec((1, MOD_TN), lambda j: (0, j)),
        ],
        out_specs=pl.BlockSpec((rows, MOD_TN), lambda j: (0, j)),
        out_shape=jax.ShapeDtypeStruct((rows, n), F32),
        compiler_params=_cparams(("arbitrary",)),
        name="modulation",
    )(c_pad, w_mod, b_mod)


_Q_TILES = Q_W // PROJ_TN
_KV_TILES = (2 * KV_W) // PROJ_TN
_U_TILES = POOL_W // PROJ_TN
_G_TILES = (2 * D_MODEL) // PROJ_TN
_KV_T0 = _Q_TILES
_U_T0 = _KV_T0 + _KV_TILES
_G_T0 = _U_T0 + _U_TILES
_PROJ_TILES = _G_T0 + _G_TILES


def _proj_kernel(x_hbm, mod_ref, g_ref, w_hbm, b_ref, q_hbm, kv_hbm, u_hbm, gl_hbm,
                 h_ref, x_v, w_v, of_v, ob_v, x_sem, w_sem, of_sem, ob_sem):
    i = pl.program_id(0)
    row0 = pl.multiple_of(i * PROJ_TM, PROJ_TM)
    n_piece = PROJ_TM // PROJ_XP

    def x_copy(p, slot):
        return pltpu.make_async_copy(
            x_hbm.at[pl.ds(row0 + p * PROJ_XP, PROJ_XP)], x_v.at[slot], x_sem.at[slot])

    def w_copy(tile, slot):
        return pltpu.make_async_copy(
            w_hbm.at[:, pl.ds(tile * PROJ_TN, PROJ_TN)], w_v.at[slot], w_sem.at[slot])

    w_copy(0, 0).start()
    x_copy(0, 0).start()
    m = mod_ref[0]
    for p in range(n_piece):
        slot = p % 2
        if p + 1 < n_piece:
            x_copy(p + 1, 1 - slot).start()
        x_copy(p, slot).wait()
        h = _rms(x_v[slot]) * g_ref[...] * (1.0 + m[1:2, :]) + m[0:1, :]
        h_ref[p * PROJ_XP:(p + 1) * PROJ_XP, :] = h.astype(BF16)

    pending = {}
    for tile in range(_PROJ_TILES):
        slot = tile % 2
        if tile + 1 < _PROJ_TILES:
            w_copy(tile + 1, 1 - slot).start()
        w_copy(tile, slot).wait()
        y = (jnp.dot(h_ref[...], w_v[slot], preferred_element_type=F32)
             + b_ref[:, tile * PROJ_TN:(tile + 1) * PROJ_TN])
        if tile < _KV_T0:
            o_hbm, t0, y = q_hbm, 0, y * ATTN_SCALE
        elif tile < _U_T0:
            o_hbm, t0 = kv_hbm, _KV_T0
        elif tile < _G_T0:
            o_hbm, t0 = u_hbm, _U_T0
        else:
            o_hbm, t0 = gl_hbm, _G_T0
        stage, sem = (ob_v, ob_sem) if o_hbm.dtype == BF16 else (of_v, of_sem)
        key = (id(stage), slot)
        if key in pending:
            pending.pop(key).wait()
        stage[slot] = y.astype(stage.dtype)
        cp = pltpu.make_async_copy(
            stage.at[slot],
            o_hbm.at[pl.ds(row0, PROJ_TM), pl.ds((tile - t0) * PROJ_TN, PROJ_TN)],
            sem.at[slot])
        cp.start()
        pending[key] = cp
    for cp in pending.values():
        cp.wait()


def _in_projection(x2, mod6, g_pre, w_in, b_in, seq):
    t, d = x2.shape
    tiles_per_seq = seq // PROJ_TM
    anyspace = pl.BlockSpec(memory_space=pl.ANY)
    return pl.pallas_call(
        _proj_kernel,
        grid=(t // PROJ_TM,),
        in_specs=[
            anyspace,
            pl.BlockSpec((1, N_MOD, d), lambda i: (i // tiles_per_seq, 0, 0)),
            pl.BlockSpec((1, d), lambda i: (0, 0)),
            anyspace,
            pl.BlockSpec(b_in.shape, lambda i: (0, 0)),
        ],
        out_specs=[anyspace, anyspace, anyspace, anyspace],
        out_shape=[
            jax.ShapeDtypeStruct((t, Q_W), BF16),
            jax.ShapeDtypeStruct((t, 2 * KV_W), BF16),
            jax.ShapeDtypeStruct((t, POOL_W), F32),
            jax.ShapeDtypeStruct((t, 2 * D_MODEL), F32),
        ],
        scratch_shapes=[pltpu.VMEM((PROJ_TM, d), BF16),
                        pltpu.VMEM((2, PROJ_XP, d), F32),
                        pltpu.VMEM((2, d, PROJ_TN), BF16),
                        pltpu.VMEM((2, PROJ_TM, PROJ_TN), F32),
                        pltpu.VMEM((2, PROJ_TM, PROJ_TN), BF16),
                        pltpu.SemaphoreType.DMA((2,)), pltpu.SemaphoreType.DMA((2,)),
                        pltpu.SemaphoreType.DMA((2,)), pltpu.SemaphoreType.DMA((2,))],
        compiler_params=_cparams(("arbitrary",)),
        name="in_projection",
    )(x2, mod6, g_pre, w_in, b_in)


def _attn_kernel(sink_ref, q_ref, kvp_ref, kvc_ref, o_ref, *, blocks_per_seq):
    i = pl.program_id(0)
    nq = ATT_TQ // WINDOW
    rows = N_Q_HEADS * WINDOW
    hrows = GROUP * WINDOW
    r_iota = lax.broadcasted_iota(jnp.int32, (rows, 2 * WINDOW), 0)
    k_iota = lax.broadcasted_iota(jnp.int32, (rows, 2 * WINDOW), 1)
    qi = r_iota % WINDOW
    band = (k_iota > qi) & (k_iota <= qi + WINDOW)
    head_of_row = lax.broadcasted_iota(jnp.int32, (rows, 1), 0) // WINDOW
    sink = jnp.zeros((rows, 1), F32)
    for qh in range(N_Q_HEADS):
        sink = jnp.where(head_of_row == qh, sink_ref[qh], sink)

    for n in range(nq):
        if n == 0:
            first = (i % blocks_per_seq) == 0
            mask = band & ((k_iota >= WINDOW) | jnp.logical_not(first))
            kv_prev = kvp_ref[...]
        else:
            mask = band
            kv_prev = kvc_ref[(n - 1) * WINDOW:n * WINDOW, :]
        kv_cur = kvc_ref[n * WINDOW:(n + 1) * WINDOW, :]
        kv = jnp.concatenate([kv_prev, kv_cur], axis=0)
        qn = q_ref[n * WINDOW:(n + 1) * WINDOW, :]
        s = jnp.concatenate(
            [lax.dot_general(
                jnp.concatenate(
                    [qn[:, (h * GROUP + g) * HEAD_DIM:(h * GROUP + g + 1) * HEAD_DIM]
                     for g in range(GROUP)], axis=0),
                kv[:, h * HEAD_DIM:(h + 1) * HEAD_DIM],
                (((1,), (1,)), ((), ())), preferred_element_type=F32)
             for h in range(N_KV_HEADS)], axis=0)
        s = jnp.where(mask, s, NEG_INF)
        m = jnp.maximum(jnp.max(s, axis=-1, keepdims=True), sink)
        p = jnp.exp(s - m)
        inv = 1.0 / (jnp.sum(p, axis=-1, keepdims=True) + jnp.exp(sink - m))
        pb = p.astype(BF16)
        for h in range(N_KV_HEADS):
            v = kv[:, KV_W + h * HEAD_DIM:KV_W + (h + 1) * HEAD_DIM]
            o = jnp.dot(pb[h * hrows:(h + 1) * hrows, :], v, preferred_element_type=F32)
            o = o * inv[h * hrows:(h + 1) * hrows, :]
            for g in range(GROUP):
                c0 = (h * GROUP + g) * HEAD_DIM
                o_ref[n * WINDOW:(n + 1) * WINDOW, c0:c0 + HEAD_DIM] = (
                    o[g * WINDOW:(g + 1) * WINDOW, :].astype(BF16))


def _attention(q, kv, sinks, seq):
    t = q.shape[0]
    blocks_per_seq = seq // ATT_TQ
    wpb = ATT_TQ // WINDOW
    return pl.pallas_call(
        functools.partial(_attn_kernel, blocks_per_seq=blocks_per_seq),
        grid=(t // ATT_TQ,),
        in_specs=[
            pl.BlockSpec(memory_space=pltpu.SMEM),
            pl.BlockSpec((ATT_TQ, Q_W), lambda i: (i, 0)),
            pl.BlockSpec((WINDOW, 2 * KV_W), lambda i: (jnp.maximum(i * wpb - 1, 0), 0)),
            pl.BlockSpec((ATT_TQ, 2 * KV_W), lambda i: (i, 0)),
        ],
        out_specs=pl.BlockSpec((ATT_TQ, Q_W), lambda i: (i, 0)),
        out_shape=jax.ShapeDtypeStruct((t, Q_W), BF16),
        compiler_params=_cparams(("arbitrary",)),
        name="swa_attention",
    )(sinks, q, kv, kv)


def _mix_kernel(x_ref, a_ref, up_ref, u_ref, gl_ref, mod_ref, wpm_ref, ps_ref, wa_ref, wp_ref,
                wo_ref, gpost_ref, gffn_ref, wrt_ref, brt_ref,
                x1_ref, h2_ref, idx_ref, gw_ref, rank_ref, cnt_ref, ubuf_ref, pool_tmp, base_ref,
                *, tiles_per_seq):
    i = pl.program_id(0)
    tm = MIX_TM
    ne = N_EXPERTS
    seq_tile = i % tiles_per_seq
    m = mod_ref[0]

    pad = SUBLANES
    top = pad + POOL_HALO
    halo = jnp.where(seq_tile == 0, 0.0, up_ref[...])
    ubuf_ref[0:pad, :] = jnp.zeros((pad, POOL_W), F32)
    ubuf_ref[pad:top, :] = halo
    ubuf_ref[top:, :] = u_ref[...]
    pool_tmp[:, 0:pad, :] = jnp.zeros((2, pad, POOL_GC), F32)
    ya = jnp.dot(a_ref[...], wa_ref[...], preferred_element_type=F32)
    pos = seq_tile * tm + lax.broadcasted_iota(jnp.int32, (tm, 1), 0)
    p_parts = []
    for g, w in enumerate(POOL_WINDOWS):
        c0 = g * POOL_GC
        ug = ubuf_ref[top:top + tm, c0:c0 + POOL_GC]
        n_stage = w.bit_length() - 1
        read = lambda lo, hi: ubuf_ref[lo:hi, c0:c0 + POOL_GC]
        for j in range(1, n_stage + 1):
            d = 1 << (j - 1)
            if j < n_stage:
                pool_tmp[j % 2, pad:top + tm, :] = read(pad, top + tm) + read(pad - d, top + tm - d)
                read = functools.partial(lambda lo, hi, sl: pool_tmp[sl, lo:hi, :], sl=j % 2)
            else:
                acc = read(top, top + tm) + read(top - d, top + tm - d)
        cnt = jnp.minimum(pos + 1, w).astype(F32)
        pg = acc / cnt - ug
        p_parts.append(jnp.dot(pg.astype(BF16), wpm_ref[g], preferred_element_type=F32))
    p = jnp.concatenate(p_parts, axis=-1) * ps_ref[...]

    yp =jnp.dot(p.astype(BF16), wp_ref[...], preferred_element_type=F32)
    gates_a = jax.nn.sigmoid(gl_ref[:, 0:D_MODEL])
    gates_p = jax.nn.sigmoid(gl_ref[:, D_MODEL:2 * D_MODEL])
    merged = gates_a * ya + gates_p * yp
    mix = jnp.dot(merged.astype(BF16), wo_ref[...], preferred_element_type=F32)
    x1 = x_ref[...] + m[2:3, :] * (_rms(mix) * gpost_ref[...])
    x1_ref[...] = x1

    h2 = _rms(x1) * gffn_ref[...] * (1.0 + m[4:5, :]) + m[3:4, :]
    h2_ref[...] = h2

    h_hi = h2.astype(BF16)
    h_lo = (h2 - h_hi.astype(F32)).astype(BF16)
    prod = lax.dot_general(wrt_ref[...], jnp.concatenate([h_hi, h_lo], axis=0),
                           (((1,), (1,)), ((), ())), preferred_element_type=F32)
    lg = ((prod[0:ne, 0:tm] + prod[ne:, 0:tm]) + (prod[0:ne, tm:] + prod[ne:, tm:])) + brt_ref[...]
    e_iota = lax.broadcasted_iota(jnp.int32, (ne, tm), 0)
    vals, idxs, picks = [], [], []
    for _ in range(TOP_K):
        mx = jnp.max(lg, axis=0, keepdims=True)
        ix = jnp.min(jnp.where(lg == mx, e_iota, ne), axis=0, keepdims=True)
        pick = e_iota == ix
        vals.append(mx)
        idxs.append(ix)
        picks.append(pick)
        lg = jnp.where(pick, -jnp.inf, lg)
    ex = [jnp.exp(v - vals[0]) for v in vals]
    den = ex[0] + ex[1] + ex[2] + ex[3]
    gw_ref[...] = jnp.concatenate([e / den for e in ex], axis=0)
    idx_ref[...] = jnp.concatenate(idxs, axis=0)

    @pl.when(i == 0)
    def _():
        base_ref[...] = jnp.zeros_like(base_ref)

    sel = (picks[0] | picks[1] | picks[2] | picks[3]).astype(F32)
    earlier = (lax.broadcasted_iota(jnp.int32, (tm, tm), 0)
               < lax.broadcasted_iota(jnp.int32, (tm, tm), 1)).astype(BF16)
    rank = jnp.dot(sel.astype(BF16), earlier, preferred_element_type=F32) + base_ref[...]
    rank_ref[...] = jnp.concatenate(
        [jnp.sum(jnp.where(pk, rank, 0.0), axis=0, keepdims=True) for pk in picks],
        axis=0).astype(jnp.int32)
    total = base_ref[...] + jnp.sum(sel, axis=1, keepdims=True)
    base_ref[...] = total
    cnt_ref[...] = total


def _split_bf16_rows(w):
    hi = w.astype(BF16)
    lo = (w - hi.astype(F32)).astype(BF16)
    return jnp.concatenate([hi, lo], axis=0)


def _mix(x2, a, u, gl, mod6, wpm, ps, wa, wp, wo, gpost, gffn, wrt, brt, seq):
    t, d = x2.shape
    tm = MIX_TM
    tiles_per_seq = seq // tm
    hb = tm // POOL_HALO
    const2 = lambda i: (0, 0)
    return pl.pallas_call(
        functools.partial(_mix_kernel, tiles_per_seq=tiles_per_seq),
        grid=(t // tm,),
        in_specs=[
            pl.BlockSpec((tm, d), lambda i: (i, 0)),
            pl.BlockSpec((tm, Q_W), lambda i: (i, 0)),
            pl.BlockSpec((POOL_HALO, POOL_W), lambda i: (jnp.maximum(i * hb - 1, 0), 0)),
            pl.BlockSpec((tm, POOL_W), lambda i: (i, 0)),
            pl.BlockSpec((tm, 2 * d), lambda i: (i, 0)),
            pl.BlockSpec((1, N_MOD, d), lambda i: (i // tiles_per_seq, 0, 0)),
            pl.BlockSpec(wpm.shape, lambda i: (0, 0, 0), pipeline_mode=pl.Buffered(1)),
            pl.BlockSpec((1, POOL_W), const2),
            pl.BlockSpec(wa.shape, const2, pipeline_mode=pl.Buffered(1)),
            pl.BlockSpec(wp.shape, const2, pipeline_mode=pl.Buffered(1)),
            pl.BlockSpec(wo.shape, const2, pipeline_mode=pl.Buffered(1)),
            pl.BlockSpec((1, d), const2),
            pl.BlockSpec((1, d), const2),
            pl.BlockSpec(wrt.shape, const2),
            pl.BlockSpec((N_EXPERTS, 1), const2),
        ],
        out_specs=[
            pl.BlockSpec((tm, d), lambda i: (i, 0)),
            pl.BlockSpec((tm, d), lambda i: (i, 0)),
            pl.BlockSpec((TOP_K, tm), lambda i: (0, i)),
            pl.BlockSpec((TOP_K, tm), lambda i: (0, i)),
            pl.BlockSpec((TOP_K, tm), lambda i: (0, i)),
            pl.BlockSpec((N_EXPERTS, 1), const2),
        ],
        out_shape=[
            jax.ShapeDtypeStruct((t, d), F32),
            jax.ShapeDtypeStruct((t, d), F32),
            jax.ShapeDtypeStruct((TOP_K, t), jnp.int32),
            jax.ShapeDtypeStruct((TOP_K, t), F32),
            jax.ShapeDtypeStruct((TOP_K, t), jnp.int32),
            jax.ShapeDtypeStruct((N_EXPERTS, 1), F32),
        ],
        scratch_shapes=[pltpu.VMEM((SUBLANES + POOL_HALO + tm, POOL_W), F32),
                        pltpu.VMEM((2, SUBLANES + POOL_HALO + tm, POOL_GC), F32),
                        pltpu.VMEM((N_EXPERTS, 1), F32)],
        compiler_params=_cparams(("arbitrary",)),
        name="mix_merge_router",
    )(x2, a, u, u, gl, mod6, wpm, ps, wa, wp, wo, gpost, gffn, wrt, brt)


def _dispatch_kernel(dest_ref, h_ref, xs_hbm, sem):
    n = DISP_TM * TOP_K

    tok_per_block = DMA_UNROLL // TOP_K

    def block(o, c):
        tok0 = pl.multiple_of(o * tok_per_block, tok_per_block)
        for j in range(DMA_UNROLL):
            pltpu.make_async_copy(
                h_ref.at[pl.ds(tok0 + j // TOP_K, 1)],
                xs_hbm.at[pl.ds(dest_ref[0, 0, o * DMA_UNROLL + j], 1)],
                sem).start(priority=j % DMA_PRIORITIES)
        return c

    lax.fori_loop(0, n // DMA_UNROLL, block, 0)
    pltpu.make_async_copy(xs_hbm.at[pl.ds(0, n)], xs_hbm.at[pl.ds(0, n)], sem).wait()


def _dispatch(dest3, h2, n_rows):
    t, d = h2.shape
    steps = t // DISP_TM
    return pl.pallas_call(
        _dispatch_kernel,
        grid=(steps,),
        in_specs=[
            pl.BlockSpec((1, 1, DISP_TM * TOP_K), lambda i: (i, 0, 0), memory_space=pltpu.SMEM),
            pl.BlockSpec((DISP_TM, d), lambda i: (i, 0)),
        ],
        out_specs=pl.BlockSpec(memory_space=pl.ANY),
        out_shape=jax.ShapeDtypeStruct((n_rows, d), F32),
        scratch_shapes=[pltpu.SemaphoreType.DMA(())],
        compiler_params=_cparams(("arbitrary",)),
        name="moe_dispatch",
    )(dest3, h2)


_MOE_NC = D_FF // MOE_FC
_GU_W = 2 * MOE_FC
assert _MOE_NC >= 4 and _MOE_NC % 2 == 0


def _moe_kernel(ie_ref, ir_ref, ins_ref, inv_ref, xs_hbm, w1_hbm, b1_ref, w2_hbm, b2_ref, y_hbm,
                xs_v, acc_v, stage_v, w1f_v, w2f_v, w1_v, w2i_v, w2_v, gu_v,
                in_sem, out_sem, w_sem):
    i = pl.program_id(0)

    @pl.when(ins_ref[i] > 0)
    def _():
        _moe_item(i, ie_ref, ir_ref, ins_ref, inv_ref, xs_hbm, w1_hbm, b1_ref, w2_hbm, b2_ref,
                  y_hbm, xs_v, acc_v, stage_v, w1f_v, w2f_v, w1_v, w2i_v, w2_v, gu_v,
                  in_sem, out_sem, w_sem)


def _moe_item(i, ie_ref, ir_ref, ins_ref, inv_ref, xs_hbm, w1_hbm, b1_ref, w2_hbm, b2_ref, y_hbm,
              xs_v, acc_v, stage_v, w1f_v, w2f_v, w1_v, w2i_v, w2_v, gu_v,
              in_sem, out_sem, w_sem):
    row0 = ir_ref[i]
    nsub = ins_ref[i]
    nvalid = inv_ref[i]
    has_next = ins_ref[i + 1] > 0
    r = MOE_R
    last_c = _MOE_NC - 1

    def rows(s):
        return pl.ds(pl.multiple_of(s * r, r), r)

    def in_copy(item, s):
        src0 = ir_ref[item] + s * r
        return pltpu.make_async_copy(
            xs_hbm.at[pl.ds(pl.multiple_of(src0, r), r)], stage_v.at[rows(s)], in_sem.at[s])

    def out_copy(s):
        return pltpu.make_async_copy(
            acc_v.at[rows(s)], y_hbm.at[pl.ds(pl.multiple_of(row0 + s * r, r), r)], out_sem)

    def weight_copies(item, c, slot):
        e = ie_ref[item]
        return (
            pltpu.make_async_copy(
                w1_hbm.at[e, :, pl.ds(pl.multiple_of(c * _GU_W, _GU_W), _GU_W)],
                w1f_v.at[slot], w_sem.at[0, slot]),
            pltpu.make_async_copy(
                w2_hbm.at[e, pl.ds(pl.multiple_of(c * MOE_FC, MOE_FC), MOE_FC), :],
                w2f_v.at[slot], w_sem.at[1, slot]),
        )

    def start_weights(item, c, slot):
        for cp in weight_copies(item, c, slot):
            cp.start()

    def wait_weights(c, slot):
        for cp in weight_copies(i, c, slot):
            cp.wait()

    def each(n, fn):
        def body(s, carry):
            fn(s)
            return carry
        lax.fori_loop(0, n, body, 0)

    def prep_w1(slot):
        w1_v[...] = w1f_v[slot].astype(BF16)

    def prep_w2(slot):
        half = LANES // 2
        for cb in range(D_MODEL // LANES):
            cs = slice(cb * LANES, (cb + 1) * LANES)
            for blk in range(MOE_FC // LANES):
                b0 = blk * LANES
                w2i_v[cb, pl.ds(b0, half, stride=2), :] = w2f_v[slot, b0:b0 + half, cs]
                w2i_v[cb, pl.ds(b0 + 1, half, stride=2), :] = w2f_v[slot, b0 + half:b0 + LANES, cs]
            w2_v[slot, :, cs] = w2i_v[cb].astype(BF16)

    lane = lax.broadcasted_iota(jnp.int32, (r, LANES), 1)
    even = (lane % 2) == 0

    def gate_up(s, c):
        b1 = b1_ref[0, :, pl.ds(pl.multiple_of(c * _GU_W, _GU_W), _GU_W)]
        return jnp.dot(xs_v[rows(s), :], w1_v[...], preferred_element_type=F32) + b1

    def activation(gu):
        glu = jnp.minimum(gu, SWIGLU_LIMIT)
        f = glu * jax.nn.sigmoid(SWIGLU_ALPHA * glu)
        lin = jnp.clip(gu, -SWIGLU_LIMIT, SWIGLU_LIMIT) + 1.0
        parts = []
        for blk in range(MOE_FC // LANES):
            ca = 2 * blk * LANES
            cb = ca + LANES
            za = f[:, ca:ca + LANES] * pltpu.roll(lin[:, ca:ca + LANES], LANES - 1, 1)
            zb = pltpu.roll(f[:, cb:cb + LANES], 1, 1) * lin[:, cb:cb + LANES]
            parts.append(jnp.where(even, za, zb))
        return jnp.concatenate(parts, axis=-1).astype(BF16)

    def load_rows(s):
        in_copy(i, s).wait()
        live = (s * r + lax.broadcasted_iota(jnp.int32, (r, 1), 0)) < nvalid
        xs_v[rows(s), :] = jnp.where(live, stage_v[rows(s), :], 0.0).astype(BF16)

    def down(s, gu, slot):
        return jnp.dot(activation(gu), w2_v[slot], preferred_element_type=F32)

    def pipelined_rows(c, slot, finish, first, tslot, after_first):
        g0 = gate_up(0, c)
        gu_v[0] = g0
        gu_v[tslot] = g0
        after_first()

        def step(s, g):
            if first:
                load_rows(s + 1)
            nxt = gate_up(s + 1, c)
            finish(s, gu_v[g], slot)
            gu_v[1 - g] = nxt
            gu_v[tslot] = nxt

        def pair(p, carry):
            step(2 * p, 0)
            step(2 * p + 1, 1)
            return carry

        def single(_, carry):
            step(nsub - 2, 0)
            return carry

        n_steps = nsub - 1
        lax.fori_loop(0, lax.shift_right_logical(n_steps, 1), pair, 0)
        lax.fori_loop(0, n_steps & 1, single, 0)

    def finish_first(s, gu, slot):
        acc_v[rows(s), :] = down(s, gu, slot)

    def finish_mid(s, gu, slot):
        acc_v[rows(s), :] = acc_v[rows(s), :] + down(s, gu, slot)

    def finish_last(s, gu, slot):
        acc_v[rows(s), :] = acc_v[rows(s), :] + down(s, gu, slot) + b2_ref[0]
        out_copy(s).start()

    tail = nsub - 1

    @pl.when(i == 0)
    def _():
        start_weights(i, 0, 0)
        each(nsub, lambda s: in_copy(i, s).start())

    wait_weights(0, 0)
    start_weights(i, 1, 1)

    @pl.when(i > 0)
    def _():
        each(ins_ref[jnp.maximum(i - 1, 0)], lambda s: out_copy(s).wait())

    prep_w1(0)
    load_rows(0)
    pipelined_rows(0, 0, finish_first, True, 2, lambda: prep_w2(0))

    @pl.when(has_next)
    def _():
        each(ins_ref[i + 1], lambda s: in_copy(i + 1, s).start())

    def mid_chunk(c, slot):
        wait_weights(c, slot)
        start_weights(i, c + 1, 1 - slot)
        prep_w1(slot)

        def after_first():
            prev = down(tail, gu_v[3 - slot], 1 - slot)
            acc_v[rows(tail), :] = jnp.where(c == 1, prev, acc_v[rows(tail), :] + prev)
            prep_w2(slot)

        pipelined_rows(c, slot, finish_mid, False, 2 + slot, after_first)

    def chunk_pair(p, carry):
        mid_chunk(2 * p + 1, 1)
        mid_chunk(2 * p + 2, 0)
        return carry

    lax.fori_loop(0, (last_c - 1) // 2, chunk_pair, 0)

    lslot = last_c & 1
    wait_weights(last_c, lslot)

    @pl.when(has_next)
    def _():
        start_weights(i + 1, 0, 1 - lslot)

    prep_w1(lslot)

    def last_after_first():
        finish_mid(tail, gu_v[3 - lslot], 1 - lslot)
        prep_w2(lslot)

    pipelined_rows(last_c, lslot, finish_last, False, 2 + lslot, last_after_first)
    finish_last(tail, gu_v[2 + lslot], lslot)

    @pl.when(jnp.logical_not(has_next))
    def _():
        each(nsub, lambda s: out_copy(s).wait())


def _moe(item_e, item_row0, item_nsub, item_nvalid, xs, w_gate_up, b_gate_up, w_down, b_down):
    n_items = item_e.shape[0]
    n_rows, d = xs.shape
    e = w_gate_up.shape[0]
    b1 = b_gate_up.reshape(e, 1, 2 * D_FF)
    b2 = b_down.reshape(e, 1, d)
    pad1 = lambda a: jnp.pad(a, (0, 1))

    grid_spec = pltpu.PrefetchScalarGridSpec(
        num_scalar_prefetch=4,
        grid=(n_items,),
        in_specs=[
            pl.BlockSpec(memory_space=pl.ANY),
            pl.BlockSpec(memory_space=pl.ANY),
            pl.BlockSpec((1, 1, 2 * D_FF), lambda i, ie, ir, ins, inv: (ie[i], 0, 0)),
            pl.BlockSpec(memory_space=pl.ANY),
            pl.BlockSpec((1, 1, d), lambda i, ie, ir, ins, inv: (ie[i], 0, 0)),
        ],
        out_specs=pl.BlockSpec(memory_space=pl.ANY),
        scratch_shapes=[
            pltpu.VMEM((MOE_TMAX, d), BF16),
            pltpu.VMEM((MOE_TMAX, d), F32),
            pltpu.VMEM((MOE_TMAX, d), F32),
            pltpu.VMEM((2, d, _GU_W), F32),
            pltpu.VMEM((2, MOE_FC, d), F32),
            pltpu.VMEM((d, _GU_W), BF16),
            pltpu.VMEM((d // LANES, MOE_FC, LANES), F32),
            pltpu.VMEM((2, MOE_FC, d), BF16),
            pltpu.VMEM((4, MOE_R, _GU_W), F32),
            pltpu.SemaphoreType.DMA((MOE_TMAX // MOE_R,)),
            pltpu.SemaphoreType.DMA(()),
            pltpu.SemaphoreType.DMA((2, 2)),
        ],
    )
    return pl.pallas_call(
        _moe_kernel,
        grid_spec=grid_spec,
        out_shape=jax.ShapeDtypeStruct((n_rows, d), F32),
        compiler_params=_cparams(("arbitrary",)),
        name="moe_experts",
    )(pad1(item_e), pad1(item_row0), pad1(item_nsub), pad1(item_nvalid),
      xs, w_gate_up, b1, w_down, b2)


def _combine_kernel(dest_ref, dest_next_ref, y_hbm, gw_ref, x1_ref, mod_ref, g_ref, o_ref,
                    buf, sem):
    i = pl.program_id(0)
    n = COMB_TM * TOP_K
    slot = i & 1

    blocks_per_k = COMB_TM // DMA_UNROLL

    def gather(d_ref, sl):
        def block(o, c):
            k = o if blocks_per_k == 1 else lax.div(o, blocks_per_k)
            tok0 = pl.multiple_of((o & (blocks_per_k - 1)) * DMA_UNROLL, DMA_UNROLL)
            for j in range(DMA_UNROLL):
                pltpu.make_async_copy(
                    y_hbm.at[pl.ds(d_ref[0, 0, o * DMA_UNROLL + j], 1)],
                    buf.at[sl, k, pl.ds(tok0 + j, 1)],
                    sem.at[sl]).start(priority=j % DMA_PRIORITIES)
            return c

        lax.fori_loop(0, n // DMA_UNROLL, block, 0)

    @pl.when(i == 0)
    def _():
        gather(dest_ref, 0)

    @pl.when(i + 1 < pl.num_programs(0))
    def _():
        gather(dest_next_ref, 1 - slot)

    pltpu.make_async_copy(buf.at[slot], buf.at[slot], sem.at[slot]).wait()

    gw = gw_ref[...]
    y = buf[slot, 0] * gw[:, 0:1]
    for k in range(1, TOP_K):
        y = y + buf[slot, k] * gw[:, k:k + 1]
    m = mod_ref[0]
    o_ref[...] = x1_ref[...] + m[5:6, :] * (_rms(y) * g_ref[...])


def _combine(dest3, yb, gw, x1, mod6, g_post, seq):
    t, d = x1.shape
    steps = t // COMB_TM
    tiles_per_seq = seq // COMB_TM
    return pl.pallas_call(
        _combine_kernel,
        grid=(steps,),
        in_specs=[
            pl.BlockSpec((1, 1, COMB_TM * TOP_K), lambda i: (i, 0, 0), memory_space=pltpu.SMEM),
            pl.BlockSpec((1, 1, COMB_TM * TOP_K), lambda i: (jnp.minimum(i + 1, steps - 1), 0, 0),
                         memory_space=pltpu.SMEM),
            pl.BlockSpec(memory_space=pl.ANY),
            pl.BlockSpec((COMB_TM, TOP_K), lambda i: (i, 0)),
            pl.BlockSpec((COMB_TM, d), lambda i: (i, 0)),
            pl.BlockSpec((1, N_MOD, d), lambda i: (i // tiles_per_seq, 0, 0)),
            pl.BlockSpec((1, d), lambda i: (0, 0)),
        ],
        out_specs=pl.BlockSpec((COMB_TM, d), lambda i: (i, 0)),
        out_shape=jax.ShapeDtypeStruct((t, d), F32),
        scratch_shapes=[pltpu.VMEM((2, TOP_K, COMB_TM, d), F32), pltpu.SemaphoreType.DMA((2,))],
        compiler_params=_cparams(("arbitrary",)),
        name="moe_combine",
    )(dest3, dest3, yb, gw, x1, mod6, g_post)


def _per_step_entries(dest_kt, tm):
    k, t = dest_kt.shape
    return dest_kt.reshape(k, t // tm, tm).transpose(1, 0, 2).reshape(t // tm, 1, k * tm)


def _routing_tables(top_idx, rank, counts, n_items):
    pcounts = (counts + MOE_R - 1) // MOE_R * MOE_R
    pend = jnp.cumsum(pcounts)
    pstart = pend - pcounts
    experts = jnp.arange(N_EXPERTS, dtype=jnp.int32)
    start_of = jnp.sum(jnp.where(top_idx[:, :, None] == experts, pstart, 0), axis=-1)
    dest = start_of + rank

    tiles = (pcounts + MOE_TMAX - 1) // MOE_TMAX
    tend = jnp.cumsum(tiles)
    tstart = tend - tiles
    it = jnp.arange(n_items, dtype=jnp.int32)
    ie = jnp.minimum(jnp.sum((tend[None, :] <= it[:, None]).astype(jnp.int32), axis=1),
                     N_EXPERTS - 1)
    valid = it < tend[-1]
    local = it - tstart[ie]
    row0 = pstart[ie] + local * MOE_TMAX
    nrows = jnp.clip(pcounts[ie] - local * MOE_TMAX, 0, MOE_TMAX)
    nsub = jnp.where(valid, nrows // MOE_R, 0).astype(jnp.int32)
    nvalid = jnp.where(valid, jnp.clip(counts[ie] - local * MOE_TMAX, 0, MOE_TMAX), 0)
    row0 = jnp.where(valid, row0, 0).astype(jnp.int32)
    return dest.astype(jnp.int32), ie, row0, nsub, nvalid.astype(jnp.int32)


def kernel(x, c, w_mod, b_mod, g_pre_mix, g_post_mix, w_in, b_in, attn_sinks, w_pool_mix,
           pool_scale, w_attn_branch, w_pool_branch, w_out, g_pre_ffn, g_post_ffn, w_router,
           b_router, w_gate_up, b_gate_up, w_down, b_down):
    b, s, d = x.shape
    t = b * s
    depth = w_mod.shape[0]
    n_assign = t * TOP_K
    n_items = N_EXPERTS + n_assign // MOE_TMAX
    n_rows = n_assign + N_EXPERTS * MOE_R

    x2 = x.reshape(t, d)
    c_pad = jnp.pad(c, ((0, SUBLANES - b), (0, 0)))
    for l in range(depth):
        mod = _modulation(c_pad, w_mod[l], b_mod[l][None, :])
        mod6 = mod[:b].reshape(b, N_MOD, d)

        q, kv, u, gl = _in_projection(x2, mod6, g_pre_mix[l][None, :], w_in[l].astype(BF16),
                                      b_in[l][None, :], s)
        a = _attention(q, kv, attn_sinks[l], s)
        x1, h2, top_idx, gate_w, rank, counts = _mix(
            x2, a, u, gl, mod6, w_pool_mix[l].astype(BF16), pool_scale[l][None, :],
            w_attn_branch[l].astype(BF16), w_pool_branch[l].astype(BF16),
            w_out[l].astype(BF16), g_post_mix[l][None, :], g_pre_ffn[l][None, :],
            _split_bf16_rows(w_router[l].T), b_router[l][:, None], s)

        dest, item_e, item_row0, item_nsub, item_nvalid = _routing_tables(
            top_idx, rank, counts[:, 0].astype(jnp.int32), n_items)
        xs = _dispatch(dest.T.reshape(t // DISP_TM, 1, DISP_TM * TOP_K), h2, n_rows)
        yb = _moe(item_e, item_row0, item_nsub, item_nvalid, xs, w_gate_up[l], b_gate_up[l],
                  w_down[l], b_down[l])
        x2 = _combine(_per_step_entries(dest, COMB_TM), yb, gate_w.T, x1, mod6,
                      g_post_ffn[l][None, :], s)
    return x2.reshape(b, s, d)
```
